```python
import math
import jax, jax.numpy as jnp
from jax import lax
import numpy as np

D_MODEL = 4096
BATCH = 2
SEQ = 4096
DEPTH = 2

ATTN_HEADS = 32
ATTN_KV_HEADS = 4
ATTN_HEAD_DIM = 64
WINDOW = 128
ATTN_BLOCK = WINDOW
ATTN_SCALE = ATTN_HEAD_DIM ** -0.5
SSM_D_INNER = 2048
SSM_HEAD_DIM = 64
SSM_HEADS = SSM_D_INNER // SSM_HEAD_DIM
SSM_GROUPS = 4
SSM_STATE = 128
SSM_CONV = 4
SSM_CHUNK = 128
SSM_XBC_W = SSM_D_INNER + 2 * SSM_GROUPS * SSM_STATE
SSM_NORM_GROUP = SSM_D_INNER // SSM_GROUPS
DT_MIN = 0.001
DT_MAX = 0.1
RET_HEADS = 8
RET_QK_DIM = 128
RET_V_DIM = 256
RET_CHUNK = 128
N_BRANCH = 3
BRANCH_WIDTH = 2048
D_FF_DENSE = 11008
N_EXPERTS = 8
TOP_K = 2
D_FF_EXPERT = 2048
N_EVEN = (DEPTH + 1) // 2
N_ODD = DEPTH // 2
DN_ALPHA = (2 * DEPTH) ** 0.25
DN_BETA = (8 * DEPTH) ** -0.25
LN_EPS = 1e-5
RMS_EPS = 1e-5

IN_SPLIT_SIZES = (
    ATTN_HEADS * ATTN_HEAD_DIM,
    ATTN_KV_HEADS * ATTN_HEAD_DIM,
    ATTN_KV_HEADS * ATTN_HEAD_DIM,
    SSM_D_INNER,
    SSM_XBC_W,
    SSM_HEADS,
    RET_HEADS * RET_QK_DIM,
    RET_HEADS * RET_QK_DIM,
    RET_HEADS * RET_V_DIM,
    RET_HEADS * RET_V_DIM,
    N_BRANCH * D_MODEL,
)
IN_WIDTH = sum(IN_SPLIT_SIZES)

kernel_name = "hybrid_swa_ssd_retention_moe_deepnorm_adaln"


def layer_norm(x, g, b):
    xf = x.astype(jnp.float32)
    mu = jnp.mean(xf, axis=-1, keepdims=True)
    var = jnp.mean(jnp.square(xf - mu), axis=-1, keepdims=True)
    return ((xf - mu) * lax.rsqrt(var + LN_EPS) * g + b).astype(x.dtype)


def alibi_slopes(n):
    return jnp.exp2(-8.0 * jnp.arange(1, n + 1, dtype=jnp.float32) / n)


def sliding_window_attention(q, k, v, sinks):
    b, s = q.shape[0], q.shape[1]
    nb = s // ATTN_BLOCK
    grp = ATTN_HEADS // ATTN_KV_HEADS
    qb = q.reshape(b, nb, ATTN_BLOCK, ATTN_KV_HEADS, grp, ATTN_HEAD_DIM)

    def band(t):
        tb = t.reshape(b, nb, ATTN_BLOCK, ATTN_KV_HEADS, ATTN_HEAD_DIM)
        prev = jnp.pad(tb, ((0, 0), (1, 0), (0, 0), (0, 0), (0, 0)))[:, :-1]
        return jnp.concatenate([prev, tb], axis=2)

    kb, vb = band(k), band(v)
    scores = jnp.einsum('bnqkgd,bnskd->bnkgqs', qb, kb).astype(jnp.float32) * ATTN_SCALE
    qi = jnp.arange(ATTN_BLOCK)[:, None]
    kj = jnp.arange(2 * ATTN_BLOCK)[None, :]
    dist = ATTN_BLOCK + qi - kj
    key_pos = jnp.arange(nb)[:, None, None] * ATTN_BLOCK - ATTN_BLOCK + kj
    valid = (dist >= 0) & (dist < WINDOW) & (key_pos >= 0)
    slopes = alibi_slopes(ATTN_HEADS).reshape(ATTN_KV_HEADS, grp)
    scores = scores - slopes[:, :, None, None] * dist.astype(jnp.float32)
    scores = jnp.where(valid[None, :, None, None], scores, -jnp.inf)
    sink = jnp.broadcast_to(
        sinks.astype(jnp.float32).reshape(ATTN_KV_HEADS, grp)[None, None, :, :, None, None],
        scores.shape[:-1] + (1,))
    probs = jax.nn.softmax(jnp.concatenate([scores, sink], axis=-1), axis=-1)[..., :-1]
    out = jnp.einsum('bnkgqs,bnskd->bnqkgd', probs.astype(vb.dtype), vb)
    return out.reshape(b, s, ATTN_HEADS * ATTN_HEAD_DIM)


def causal_depthwise_conv(u, w, bias):
    out = lax.conv_general_dilated(
        u, w[:, None, :].astype(u.dtype), window_strides=(1,), padding=[(SSM_CONV - 1, 0)],
        dimension_numbers=('NWC', 'WIO', 'NWC'), feature_group_count=u.shape[-1])
    return out + bias


def ssd_chunked(xs, dt, a, bm, cm):
    b, s = xs.shape[0], xs.shape[1]
    nc = s // SSM_CHUNK
    hg = SSM_HEADS // SSM_GROUPS
    Q = SSM_CHUNK
    x = (xs.astype(jnp.float32) * dt[..., None]).reshape(b, nc, Q, SSM_GROUPS, hg, SSM_HEAD_DIM)
    la = jnp.moveaxis((dt * a).reshape(b, nc, Q, SSM_GROUPS, hg), 2, -1)
    cs = jnp.cumsum(la, axis=-1)
    bc = bm.reshape(b, nc, Q, SSM_GROUPS, SSM_STATE).astype(jnp.float32)
    cc = cm.reshape(b, nc, Q, SSM_GROUPS, SSM_STATE).astype(jnp.float32)
    causal = jnp.tril(jnp.ones((Q, Q), dtype=bool))
    lmat = jnp.exp(jnp.where(causal, cs[..., :, None] - cs[..., None, :], -jnp.inf))
    cb = jnp.einsum('bcqgn,bcsgn->bcgqs', cc, bc)
    y_diag = jnp.einsum('bcgqs,bcghqs,bcsghp->bcqghp', cb, lmat, x)
    decay_to_end = jnp.exp(cs[..., -1:] - cs)
    states = jnp.einsum('bcsgn,bcghs,bcsghp->bcghpn', bc, decay_to_end, x)
    chunk_decay = jnp.exp(cs[..., -1])

    def step(h, inp):
        dec, st = inp
        return h * dec[..., None, None] + st, h

    h0 = jnp.zeros((b, SSM_GROUPS, hg, SSM_HEAD_DIM, SSM_STATE), jnp.float32)
    _, h_prev = lax.scan(step, h0, (jnp.moveaxis(chunk_decay, 1, 0), jnp.moveaxis(states, 1, 0)))
    h_prev = jnp.moveaxis(h_prev, 0, 1)
    y_off = jnp.einsum('bcqgn,bcghpn,bcghq->bcqghp', cc, h_prev, jnp.exp(cs))
    return (y_diag + y_off).reshape(b, s, SSM_HEADS, SSM_HEAD_DIM)


def mamba2_branch(z, xbc, dt_raw, conv_w, conv_b, dt_bias, a_log, d_skip, norm_w):
    b, s = z.shape[0], z.shape[1]
    xbc = jax.nn.silu(causal_depthwise_conv(xbc, conv_w, conv_b))
    xs, bm, cm = jnp.split(xbc, [SSM_D_INNER, SSM_D_INNER + SSM_GROUPS * SSM_STATE], axis=-1)
    xs = xs.reshape(b, s, SSM_HEADS, SSM_HEAD_DIM)
    bm = bm.reshape(b, s, SSM_GROUPS, SSM_STATE)
    cm = cm.reshape(b, s, SSM_GROUPS, SSM_STATE)
    dt = jax.nn.softplus((dt_raw + dt_bias).astype(jnp.float32))
    a = -jnp.exp(a_log.astype(jnp.float32))
    y = ssd_chunked(xs, dt, a, bm, cm) + d_skip.astype(jnp.float32)[:, None] * xs.astype(jnp.float32)
    yg = (y.reshape(b, s, SSM_D_INNER) * jax.nn.silu(z.astype(jnp.float32))).reshape(b, s, SSM_GROUPS, SSM_NORM_GROUP)
    yg = yg * lax.rsqrt(jnp.mean(jnp.square(yg), axis=-1, keepdims=True) + RMS_EPS)
    return (yg.reshape(b, s, SSM_D_INNER) * norm_w).astype(z.dtype)


def retention_chunkwise(q, k, v):
    b, s = q.shape[0], q.shape[1]
    nc = s // RET_CHUNK
    C = RET_CHUNK
    log_gamma = jnp.log1p(-jnp.exp2(-5.0 - jnp.arange(RET_HEADS, dtype=jnp.float32)))
    qc = q.reshape(b, nc, C, RET_HEADS, RET_QK_DIM)
    kc = (k * (RET_QK_DIM ** -0.5)).reshape(b, nc, C, RET_HEADS, RET_QK_DIM)
    vc = v.reshape(b, nc, C, RET_HEADS, RET_V_DIM)
    pos = jnp.arange(C, dtype=jnp.float32)
    diff = pos[:, None] - pos[None, :]
    decay_mask = jnp.where(diff >= 0, jnp.exp(log_gamma[:, None, None] * jnp.maximum(diff, 0.0)), 0.0)
    scores = jnp.einsum('bcqhd,bcshd->bchqs', qc, kc).astype(jnp.float32) * decay_mask
    inner = jnp.einsum('bchqs,bcshv->bcqhv', scores, vc.astype(jnp.float32))
    k_decay = jnp.exp((C - 1.0 - pos)[:, None] * log_gamma[None, :])
    kv = jnp.einsum('bcshd,sh,bcshv->bchdv', kc.astype(jnp.float32), k_decay, vc.astype(jnp.float32))
    chunk_decay = jnp.exp(log_gamma * C)

    def step(r, kv_c):
        return r * chunk_decay[None, :, None, None] + kv_c, r

    r0 = jnp.zeros((b, RET_HEADS, RET_QK_DIM, RET_V_DIM), jnp.float32)
    _, r_prev = lax.scan(step, r0, jnp.moveaxis(kv, 1, 0))
    r_prev = jnp.moveaxis(r_prev, 0, 1)
    q_decay = jnp.exp((pos + 1.0)[:, None] * log_gamma[None, :])
    cross = jnp.einsum('bcqhd,bchdv,qh->bcqhv', qc.astype(jnp.float32), r_prev, q_decay)
    return (inner + cross).reshape(b, s, RET_HEADS, RET_V_DIM)


def retention_branch(q, k, v, g):
    b, s = q.shape[0], q.shape[1]
    y = retention_chunkwise(q.reshape(b, s, RET_HEADS, RET_QK_DIM), k.reshape(b, s, RET_HEADS, RET_QK_DIM),
                            v.reshape(b, s, RET_HEADS, RET_V_DIM))
    mu = jnp.mean(y, axis=-1, keepdims=True)
    var = jnp.mean(jnp.square(y - mu), axis=-1, keepdims=True)
    y = ((y - mu) * lax.rsqrt(var + LN_EPS)).reshape(b, s, RET_HEADS * RET_V_DIM)
    return (jax.nn.silu(g.astype(jnp.float32)) * y).astype(q.dtype)


def token_mixers(u, w_in, sinks, conv_w, conv_b, dt_bias, a_log, d_skip, norm_w, w_branch, w_o):
    b, s = u.shape[0], u.shape[1]
    proj = u @ w_in
    (aq, ak, av, z, xbc, dt_raw, rq, rk, rv, rg, gates) = jnp.split(
        proj, np.cumsum(IN_SPLIT_SIZES)[:-1].tolist(), axis=-1)
    y_attn = sliding_window_attention(
        aq.reshape(b, s, ATTN_HEADS, ATTN_HEAD_DIM), ak.reshape(b, s, ATTN_KV_HEADS, ATTN_HEAD_DIM),
        av.reshape(b, s, ATTN_KV_HEADS, ATTN_HEAD_DIM), sinks)
    y_ssm = mamba2_branch(z, xbc, dt_raw, conv_w, conv_b, dt_bias, a_log, d_skip, norm_w)
    y_ret = retention_branch(rq, rk, rv, rg)
    ys = jnp.stack([y_attn.astype(u.dtype), y_ssm, y_ret], axis=2)
    wide = jnp.einsum('bsrw,rwd->bsrd', ys, w_branch)
    merged = jnp.sum(jax.nn.sigmoid(gates.reshape(b, s, N_BRANCH, D_MODEL)) * wide, axis=2)
    return merged @ w_o


def swiglu(u, w1, w3, w2):
    return (jax.nn.silu(u @ w1) * (u @ w3)) @ w2


def moe_swiglu(u, router_w, w1, w3, w2):
    logits = (u @ router_w).astype(jnp.float32)
    top_vals, top_idx = lax.top_k(logits, TOP_K)
    top_w = jax.nn.softmax(top_vals, axis=-1)
    combine = jnp.sum(jax.nn.one_hot(top_idx, N_EXPERTS, dtype=jnp.float32) * top_w[..., None], axis=-2)
    combine = combine.astype(u.dtype)
    out = jnp.zeros_like(u)
    for e in range(N_EXPERTS):
        out = out + combine[..., e:e + 1] * swiglu(u, w1[e], w3[e], w2[e])
    return out


def setup_inputs(seed: int = 0) -> dict:
    key = jax.random.key(seed)
    ks = jax.random.split(key, 24)
    f32 = jnp.float32
    d = D_MODEL

    def nrm(k, shape, scale):
        return jax.random.normal(k, shape, f32) * scale

    dt0 = jnp.exp(jax.random.uniform(ks[8], (DEPTH, SSM_HEADS), f32, math.log(DT_MIN), math.log(DT_MAX)))
    return {
        "x": nrm(ks[0], (BATCH, SEQ, d), 1.0),
        "c": nrm(ks[1], (BATCH, d), 1.0),
        "w_ada": nrm(ks[2], (DEPTH, d, 6 * d), d ** -0.5),
        "b_ada": nrm(ks[3], (DEPTH, 6 * d), 0.02),
        "w_in": nrm(ks[4], (DEPTH, d, IN_WIDTH), d ** -0.5),
        "attn_sinks": nrm(ks[5], (DEPTH, ATTN_HEADS), 0.5),
        "conv_w": nrm(ks[6], (DEPTH, SSM_CONV, SSM_XBC_W), SSM_CONV ** -0.5),
        "conv_b": nrm(ks[7], (DEPTH, SSM_XBC_W), 0.02),
        "dt_bias": dt0 + jnp.log(-jnp.expm1(-dt0)),
        "a_log": jnp.log(jax.random.uniform(ks[9], (DEPTH, SSM_HEADS), f32, 1.0, 16.0)),
        "d_skip": 1.0 + nrm(ks[10], (DEPTH, SSM_HEADS), 0.1),
        "ssm_norm_w": 1.0 + nrm(ks[11], (DEPTH, SSM_D_INNER), 0.02),
        "w_branch": nrm(ks[12], (DEPTH, N_BRANCH, BRANCH_WIDTH, d), BRANCH_WIDTH ** -0.5),
        "w_o": nrm(ks[13], (DEPTH, d, d), DN_BETA * d ** -0.5),
        "ln_g": 1.0 + nrm(ks[14], (DEPTH, 2, d), 0.02),
        "ln_b": nrm(ks[15], (DEPTH, 2, d), 0.02),
        "ffn_w1": nrm(ks[16], (N_EVEN, d, D_FF_DENSE), d ** -0.5),
        "ffn_w3": nrm(ks[17], (N_EVEN, d, D_FF_DENSE), d ** -0.5),
        "ffn_w2": nrm(ks[18], (N_EVEN, D_FF_DENSE, d), DN_BETA * D_FF_DENSE ** -0.5),
        "router_w": nrm(ks[19], (N_ODD, d, N_EXPERTS), d ** -0.5),
        "exp_w1": nrm(ks[20], (N_ODD, N_EXPERTS, d, D_FF_EXPERT), d ** -0.5),
        "exp_w3": nrm(ks[21], (N_ODD, N_EXPERTS, d, D_FF_EXPERT), d ** -0.5),
        "exp_w2": nrm(ks[22], (N_ODD, N_EXPERTS, D_FF_EXPERT, d), DN_BETA * D_FF_EXPERT ** -0.5),
    }


def reference(x, c, w_ada, b_ada, w_in, attn_sinks, conv_w, conv_b, dt_bias, a_log, d_skip, ssm_norm_w,
              w_branch, w_o, ln_g, ln_b, ffn_w1, ffn_w3, ffn_w2, router_w, exp_w1, exp_w3, exp_w2):
    cond = jax.nn.silu(c)
    for l in range(DEPTH):
        mod = cond @ w_ada[l] + b_ada[l]
        sh1, sc1, g1, sh2, sc2, g2 = jnp.split(mod, 6, axis=-1)
        u = x * (1.0 + sc1[:, None, :]) + sh1[:, None, :]
        mix = token_mixers(u, w_in[l], attn_sinks[l], conv_w[l], conv_b[l], dt_bias[l], a_log[l],
                           d_skip[l], ssm_norm_w[l], w_branch[l], w_o[l])
        x = layer_norm(DN_ALPHA * x + g1[:, None, :] * mix, ln_g[l, 0], ln_b[l, 0])
        u = x * (1.0 + sc2[:, None, :]) + sh2[:, None, :]
        if l % 2 == 0:
            f = swiglu(u, ffn_w1[l // 2], ffn_w3[l // 2], ffn_w2[l // 2])
        else:
            f = moe_swiglu(u, router_w[l // 2], exp_w1[l // 2], exp_w3[l // 2], exp_w2[l // 2])
        x = layer_norm(DN_ALPHA * x + g2[:, None, :] * f, ln_g[l, 1], ln_b[l, 1])
    return x
```

```python
import functools
import math

import jax
import jax.numpy as jnp
from jax import lax
from jax.experimental import pallas as pl
from jax.experimental.pallas import tpu as pltpu

F32 = jnp.float32
BF = jnp.bfloat16
HIGHEST = lax.Precision.HIGHEST

D_MODEL = 4096
DEPTH = 2
ATTN_HEADS = 32
ATTN_KV_HEADS = 4
ATTN_GROUP = ATTN_HEADS // ATTN_KV_HEADS
ATTN_HEAD_DIM = 64
WINDOW = 128
ATTN_SCALE = ATTN_HEAD_DIM ** -0.5
SSM_D_INNER = 2048
SSM_HEAD_DIM = 64
SSM_HEADS = SSM_D_INNER // SSM_HEAD_DIM
SSM_GROUPS = 4
SSM_STATE = 128
SSM_CONV = 4
SSM_CHUNK = 128
SSM_XBC_W = SSM_D_INNER + 2 * SSM_GROUPS * SSM_STATE
SSM_NORM_GROUP = SSM_D_INNER // SSM_GROUPS
RET_HEADS = 8
RET_QK_DIM = 128
RET_V_DIM = 256
RET_CHUNK = 128
N_BRANCH = 3
BRANCH_WIDTH = 2048
D_FF_DENSE = 11008
N_EXPERTS = 8
D_FF_EXPERT = 2048
DN_ALPHA = (2 * DEPTH) ** 0.25
LN_EPS = 1e-5
RMS_EPS = 1e-5

OFF_Q = 0
OFF_KV = OFF_Q + ATTN_HEADS * ATTN_HEAD_DIM
OFF_Z = OFF_KV + 2 * ATTN_KV_HEADS * ATTN_HEAD_DIM
OFF_XBC = OFF_Z + SSM_D_INNER
OFF_DT = OFF_XBC + SSM_XBC_W
OFF_TAIL = OFF_DT + SSM_HEADS
RET_WIDTH = 2 * RET_HEADS * RET_QK_DIM + 2 * RET_HEADS * RET_V_DIM
OFF_GATES_IN_TAIL = RET_WIDTH

LANES = 128
SUBLANES = 8
VMEM_LIMIT_CAP = 56 << 20
MASK_VALUE = -1e30


def _cparams(semantics, vmem_bytes):
    limit = int(min(max(vmem_bytes + (8 << 20), 24 << 20), VMEM_LIMIT_CAP))
    return pltpu.CompilerParams(dimension_semantics=semantics, vmem_limit_bytes=limit)


def _sigmoid(v):
    return 1.0 / (1.0 + jnp.exp(-v))


def _softplus(v):
    return jnp.maximum(v, 0.0) + jnp.log(1.0 + jnp.exp(-jnp.abs(v)))


def _mod_kernel(c_ref, w_ref, b_ref, o_ref):
    c = c_ref[...]
    cond = c * _sigmoid(c)
    o_ref[...] = jnp.dot(cond.astype(BF), w_ref[...].astype(BF),
                         preferred_element_type=F32) + b_ref[...]


def _adaln_mod(c, w_ada, b_ada):
    nb = c.shape[0]
    depth, d, n = w_ada.shape
    tn = 512
    c8 = jnp.pad(c, ((0, SUBLANES - nb), (0, 0)))
    out = pl.pallas_call(
        _mod_kernel,
        grid=(depth, n // tn),
        in_specs=[
            pl.BlockSpec((SUBLANES, d), lambda l, j: (0, 0)),
            pl.BlockSpec((None, d, tn), lambda l, j: (l, 0, j)),
            pl.BlockSpec((None, 1, tn), lambda l, j: (l, 0, j)),
        ],
        out_specs=pl.BlockSpec((None, SUBLANES, tn), lambda l, j: (l, 0, j)),
        out_shape=jax.ShapeDtypeStruct((depth, SUBLANES, n), F32),
        compiler_params=_cparams(("arbitrary", "arbitrary"), 2 * d * tn * 4 + d * tn * 2),
        name="adaln_mod",
    )(c8, w_ada, b_ada.reshape(depth, 1, n))
    return out[:, :nb, :]


def _modulate_kernel(x_ref, sc_ref, sh_ref, o_ref):
    o_ref[...] = (x_ref[...] * (1.0 + sc_ref[...]) + sh_ref[...]).astype(BF)


def _modulate(x2, sc, sh, seq):
    t, d = x2.shape
    tm = 512
    per_b = seq // tm
    vec = pl.BlockSpec((None, 1, d), lambda i: (i // per_b, 0, 0))
    return pl.pallas_call(
        _modulate_kernel,
        grid=(t // tm,),
        in_specs=[pl.BlockSpec((tm, d), lambda i: (i, 0)), vec, vec],
        out_specs=pl.BlockSpec((tm, d), lambda i: (i, 0)),
        out_shape=jax.ShapeDtypeStruct((t, d), BF),
        compiler_params=_cparams(("arbitrary",), 2 * tm * d * 6),
        name="modulate",
    )(x2, sc, sh)


def _ln_kernel(x_ref, y_ref, gate_ref, g_ref, b_ref, *rest, emit_u):
    r = DN_ALPHA * x_ref[...] + gate_ref[...] * y_ref[...]
    mu = jnp.mean(r, axis=-1, keepdims=True)
    dlt = r - mu
    var = jnp.mean(dlt * dlt, axis=-1, keepdims=True)
    xn = dlt * lax.rsqrt(var + LN_EPS) * g_ref[...] + b_ref[...]
    if emit_u:
        sc_ref, sh_ref, xo_ref, uo_ref = rest
        xo_ref[...] = xn
        uo_ref[...] = (xn * (1.0 + sc_ref[...]) + sh_ref[...]).astype(BF)
    else:
        (xo_ref,) = rest
        xo_ref[...] = xn


def _residual_ln(x2, y2, gate, ln_g, ln_b, seq, sc=None, sh=None):
    t, d = x2.shape
    tm = 256
    per_b = seq // tm
    emit_u = sc is not None
    row = pl.BlockSpec((tm, d), lambda i: (i, 0))
    vec = pl.BlockSpec((None, 1, d), lambda i: (i // per_b, 0, 0))
    par = pl.BlockSpec((1, d), lambda i: (0, 0))
    in_specs = [row, row, vec, par, par]
    args = [x2, y2, gate, ln_g.reshape(1, d), ln_b.reshape(1, d)]
    out_specs = [row]
    out_shape = [jax.ShapeDtypeStruct((t, d), F32)]
    if emit_u:
        in_specs += [vec, vec]
        args += [sc, sh]
        out_specs.append(row)
        out_shape.append(jax.ShapeDtypeStruct((t, d), BF))
    outs = pl.pallas_call(
        functools.partial(_ln_kernel, emit_u=emit_u),
        grid=(t // tm,),
        in_specs=in_specs,
        out_specs=out_specs,
        out_shape=out_shape,
        compiler_params=_cparams(("arbitrary",), 2 * tm * d * 14 + 4 * tm * d * 4),
        name="residual_ln",
    )(*args)
    return outs if emit_u else (outs[0], None)


def _mm_ws_kernel(a_ref, b_ref, o_ref, *scratch):
    if scratch:
        (wbf,) = scratch

        @pl.when(pl.program_id(1) == 0)
        def _():
            wbf[...] = b_ref[...].astype(BF)

        w = wbf[...]
    else:
        w = b_ref[...]
    o_ref[...] = jnp.dot(a_ref[...], w, preferred_element_type=F32).astype(o_ref.dtype)


def _matmul_ws(a, w, lead, col0, n, out_dtype, tm, tn, name):
    m, k = a.shape
    joff = col0 // tn
    assert col0 % tn == 0 and n % tn == 0 and m % tm == 0
    nlead = len(lead)
    w_spec = pl.BlockSpec((None,) * nlead + (k, tn), lambda j, i: tuple(lead) + (0, j + joff))
    cast = w.dtype != BF
    scratch = [pltpu.VMEM((k, tn), BF)] if cast else []
    osz = jnp.dtype(out_dtype).itemsize
    vmem = 2 * tm * k * 2 + 2 * k * tn * w.dtype.itemsize + k * tn * 2 * cast + 2 * tm * tn * osz + tm * tn * 4
    return pl.pallas_call(
        _mm_ws_kernel,
        grid=(n // tn, m // tm),
        in_specs=[pl.BlockSpec((tm, k), lambda j, i: (i, 0)), w_spec],
        out_specs=pl.BlockSpec((tm, tn), lambda j, i: (i, j)),
        out_shape=jax.ShapeDtypeStruct((m, n), out_dtype),
        scratch_shapes=scratch,
        compiler_params=_cparams(("arbitrary", "arbitrary"), vmem),
        name=name,
    )(a, w)


def _mm_k_kernel(a_ref, b_ref, o_ref, acc_ref):
    kk = pl.program_id(2)

    @pl.when(kk == 0)
    def _():
        acc_ref[...] = jnp.zeros_like(acc_ref)

    acc_ref[...] += jnp.dot(a_ref[...], b_ref[...].astype(BF), preferred_element_type=F32)

    @pl.when(kk == pl.num_programs(2) - 1)
    def _():
        o_ref[...] = acc_ref[...].astype(o_ref.dtype)


def _matmul_k(a, w, lead, out_dtype, tm, tn, tk, name):
    m, k = a.shape
    n = w.shape[-1]
    assert m % tm == 0 and n % tn == 0 and k % tk == 0
    nlead = len(lead)
    w_spec = pl.BlockSpec((None,) * nlead + (tk, tn), lambda i, j, kk: tuple(lead) + (kk, j))
    osz = jnp.dtype(out_dtype).itemsize
    vmem = 2 * tm * tk * 2 + 2 * tk * tn * 4 + tk * tn * 2 + 2 * tm * tn * osz + 2 * tm * tn * 4
    return pl.pallas_call(
        _mm_k_kernel,
        grid=(m // tm, n // tn, k // tk),
        in_specs=[pl.BlockSpec((tm, tk), lambda i, j, kk: (i, kk)), w_spec],
        out_specs=pl.BlockSpec((tm, tn), lambda i, j, kk: (i, j)),
        out_shape=jax.ShapeDtypeStruct((m, n), out_dtype),
        scratch_shapes=[pltpu.VMEM((tm, tn), F32)],
        compiler_params=_cparams(("arbitrary", "arbitrary", "arbitrary"), vmem),
        name=name,
    )(a, w)


def _attn_kernel(q_ref, kp_ref, kc_ref, vp_ref, vc_ref, bp_ref, bc_ref, sink_ref, o_ref):
    nblk = pl.program_id(2)
    grp, blk, dh = q_ref.shape
    q = q_ref[...].reshape(grp * blk, dh)
    dims = (((1,), (1,)), ((), ()))
    sp = lax.dot_general(q, kp_ref[...], dims, preferred_element_type=F32).reshape(grp, blk, blk)
    sc = lax.dot_general(q, kc_ref[...], dims, preferred_element_type=F32).reshape(grp, blk, blk)
    sp = sp * ATTN_SCALE + bp_ref[...]
    sp = jnp.where(nblk == 0, MASK_VALUE, sp)
    sc = sc * ATTN_SCALE + bc_ref[...]
    sink = sink_ref[...][:, :, :1]
    m = jnp.maximum(jnp.maximum(jnp.max(sp, axis=-1, keepdims=True),
                                jnp.max(sc, axis=-1, keepdims=True)), sink)
    pp = jnp.exp(sp - m)
    pc = jnp.exp(sc - m)
    den = jnp.sum(pp, axis=-1, keepdims=True) + jnp.sum(pc, axis=-1, keepdims=True) + jnp.exp(sink - m)
    o = jnp.dot(pp.astype(BF).reshape(grp * blk, blk), vp_ref[...], preferred_element_type=F32)
    o = o + jnp.dot(pc.astype(BF).reshape(grp * blk, blk), vc_ref[...], preferred_element_type=F32)
    o_ref[...] = (o.reshape(grp, blk, dh) / den).astype(o_ref.dtype)


def _attention(q2, kv2, sinks, nb, seq):
    blk = WINDOW
    nblk = seq // blk
    q = q2.reshape(nb, seq, ATTN_KV_HEADS, ATTN_GROUP, ATTN_HEAD_DIM).transpose(0, 2, 3, 1, 4)
    kvw = ATTN_KV_HEADS * ATTN_HEAD_DIM
    k = kv2[:, :kvw].reshape(nb, seq, ATTN_KV_HEADS, ATTN_HEAD_DIM).transpose(0, 2, 1, 3)
    v = kv2[:, kvw:].reshape(nb, seq, ATTN_KV_HEADS, ATTN_HEAD_DIM).transpose(0, 2, 1, 3)
    slopes = jnp.exp2(-8.0 * jnp.arange(1, ATTN_HEADS + 1, dtype=F32) / ATTN_HEADS)
    slopes = slopes.reshape(ATTN_KV_HEADS, ATTN_GROUP, 1, 1)
    qi = jnp.arange(blk)[:, None]
    kj = jnp.arange(blk)[None, :]
    dist_p = blk + qi - kj
    dist_c = qi - kj
    bias_p = jnp.where(dist_p < WINDOW, -slopes * dist_p.astype(F32), MASK_VALUE)
    bias_c = jnp.where(dist_c >= 0, -slopes * dist_c.astype(F32), MASK_VALUE)
    sink_t = jnp.broadcast_to(sinks.astype(F32).reshape(ATTN_KV_HEADS, ATTN_GROUP, 1, 1),
                              (ATTN_KV_HEADS, ATTN_GROUP, 1, LANES))
    q_spec = pl.BlockSpec((None, None, ATTN_GROUP, blk, ATTN_HEAD_DIM), lambda b, h, n: (b, h, 0, n, 0))
    cur = pl.BlockSpec((None, None, blk, ATTN_HEAD_DIM), lambda b, h, n: (b, h, n, 0))
    prev = pl.BlockSpec((None, None, blk, ATTN_HEAD_DIM), lambda b, h, n: (b, h, jnp.maximum(n - 1, 0), 0))
    tab = pl.BlockSpec((None, ATTN_GROUP, blk, blk), lambda b, h, n: (h, 0, 0, 0))
    sk = pl.BlockSpec((None, ATTN_GROUP, 1, LANES), lambda b, h, n: (h, 0, 0, 0))
    o = pl.pallas_call(
        _attn_kernel,
        grid=(nb, ATTN_KV_HEADS, nblk),
        in_specs=[q_spec, prev, cur, prev, cur, tab, tab, sk],
        out_specs=q_spec,
        out_shape=jax.ShapeDtypeStruct(q.shape, BF),
        compiler_params=_cparams(("arbitrary", "arbitrary", "arbitrary"), 16 << 20),
        name="swa_attention",
    )(q, k, k, v, v, bias_p, bias_c, sink_t)
    return o.transpose(0, 3, 1, 2, 4).reshape(nb * seq, ATTN_HEADS * ATTN_HEAD_DIM)


def _ssd_kernel(z_ref, x_ref, dt_ref, dtt_ref, cw_ref, cb_ref, dtb_ref, dtbt_ref, al_ref, alt_ref,
                dsk_ref, nw_ref, e_ref, o_ref, xbuf, state, yscr):
    q_len = SSM_CHUNK
    halo = SUBLANES

    @pl.when(pl.program_id(1) == 0)
    def _():
        xbuf[0:halo, :] = jnp.zeros((halo, SSM_XBC_W), F32)
        state[...] = jnp.zeros_like(state)

    xbuf[halo:halo + q_len, :] = x_ref[...].astype(F32)
    acc = cb_ref[...] + cw_ref[0:1, :] * xbuf[halo - 3:halo - 3 + q_len, :]
    for j in range(1, SSM_CONV):
        acc = acc + cw_ref[j:j + 1, :] * xbuf[halo - 3 + j:halo - 3 + j + q_len, :]
    xbuf[0:halo, :] = xbuf[q_len:q_len + halo, :]
    xbc = acc * _sigmoid(acc)
    xs = xbc[:, :SSM_D_INNER]
    bm = xbc[:, SSM_D_INNER:SSM_D_INNER + SSM_GROUPS * SSM_STATE]
    cm = xbc[:, SSM_D_INNER + SSM_GROUPS * SSM_STATE:]

    dt = _softplus(dt_ref[...] + dtb_ref[...])
    la = dt * (-jnp.exp(al_ref[...]))
    la_t = _softplus(dtt_ref[...] + dtbt_ref[...]) * (-jnp.exp(alt_ref[...]))
    row = lax.broadcasted_iota(jnp.int32, (q_len, q_len), 0)
    col = lax.broadcasted_iota(jnp.int32, (q_len, q_len), 1)
    causal = col <= row
    cs = jnp.dot(causal.astype(F32), la, precision=HIGHEST, preferred_element_type=F32)
    cs_t = jnp.dot(la_t, (row <= col).astype(F32), precision=HIGHEST, preferred_element_type=F32)
    ecs = jnp.exp(cs)
    dte = jnp.exp(cs[q_len - 1:q_len, :] - cs)
    stack = jnp.concatenate([dt, dt * dte, ecs], axis=0)
    wide = jnp.dot(stack, e_ref[...], precision=HIGHEST, preferred_element_type=F32)
    w_dt = wide[0:q_len]
    w_dd = wide[q_len:2 * q_len]
    w_e = wide[2 * q_len:3 * q_len]
    x_dt = (xs * w_dt).astype(BF)
    x_dd = (xs * w_dd).astype(BF)
    left = lax.broadcasted_iota(jnp.int32, (q_len, LANES), 1) < SSM_HEAD_DIM
    heads_per_group = SSM_HEADS // SSM_GROUPS
    for g in range(SSM_GROUPS):
        bg = bm[:, g * SSM_STATE:(g + 1) * SSM_STATE]
        cg = cm[:, g * SSM_STATE:(g + 1) * SSM_STATE].astype(BF)
        cbm = lax.dot_general(cg, bg.astype(BF), (((1,), (1,)), ((), ())), preferred_element_type=F32)
        bg_t = bg.T.astype(BF)
        for jp in range(heads_per_group // 2):
            p = g * (heads_per_group // 2) + jp
            h0 = 2 * p
            sl = slice(p * LANES, (p + 1) * LANES)
            l0 = jnp.exp(jnp.where(causal, cs[:, h0:h0 + 1] - cs_t[h0:h0 + 1, :], -jnp.inf))
            l1 = jnp.exp(jnp.where(causal, cs[:, h0 + 1:h0 + 2] - cs_t[h0 + 1:h0 + 2, :], -jnp.inf))
            xp = x_dt[:, sl]
            y0 = jnp.dot((cbm * l0).astype(BF), xp, preferred_element_type=F32)
            y1 = jnp.dot((cbm * l1).astype(BF), xp, preferred_element_type=F32)
            st = state[p]
            y_off = jnp.dot(cg, st.astype(BF), preferred_element_type=F32)
            yscr[:, sl] = jnp.where(left, y0, y1) + y_off * w_e[:, sl] + dsk_ref[:, sl] * xs[:, sl]
            state[p] = st * w_e[q_len - 1:q_len, sl] + jnp.dot(bg_t, x_dd[:, sl], preferred_element_type=F32)

    zf = z_ref[...].astype(F32)
    yg = yscr[...] * (zf * _sigmoid(zf))
    for g in range(SSM_GROUPS):
        gs = slice(g * SSM_NORM_GROUP, (g + 1) * SSM_NORM_GROUP)
        blk = yg[:, gs]
        ms = jnp.mean(blk * blk, axis=-1, keepdims=True)
        o_ref[:, gs] = (blk * lax.rsqrt(ms + RMS_EPS) * nw_ref[:, gs]).astype(o_ref.dtype)


def _ssd(z2, xbc2, dt2, conv_w, conv_b, dt_bias, a_log, d_skip, norm_w, nb, seq):
    t = nb * seq
    q_len = SSM_CHUNK
    nchunk = seq // q_len
    dt_t = dt2.reshape(nb, seq, SSM_HEADS).transpose(0, 2, 1)
    expand = jnp.repeat(jnp.eye(SSM_HEADS, dtype=F32), SSM_HEAD_DIM, axis=1)
    dsk = jnp.repeat(d_skip.astype(F32), SSM_HEAD_DIM).reshape(1, SSM_D_INNER)
    rowblk = lambda w: pl.BlockSpec((q_len, w), lambda b, c: (b * nchunk + c, 0))
    full = lambda r, w: pl.BlockSpec((r, w), lambda b, c: (0, 0))
    return pl.pallas_call(
        _ssd_kernel,
        grid=(nb, nchunk),
        in_specs=[
            rowblk(SSM_D_INNER), rowblk(SSM_XBC_W), rowblk(SSM_HEADS),
            pl.BlockSpec((None, SSM_HEADS, q_len), lambda b, c: (b, 0, c)),
            full(SSM_CONV, SSM_XBC_W), full(1, SSM_XBC_W),
            full(1, SSM_HEADS), full(SSM_HEADS, 1), full(1, SSM_HEADS), full(SSM_HEADS, 1),
            full(1, SSM_D_INNER), full(1, SSM_D_INNER), full(SSM_HEADS, SSM_D_INNER),
        ],
        out_specs=rowblk(SSM_D_INNER),
        out_shape=jax.ShapeDtypeStruct((t, SSM_D_INNER), BF),
        scratch_shapes=[
            pltpu.VMEM((q_len + 2 * SUBLANES, SSM_XBC_W), F32),
            pltpu.VMEM((SSM_HEADS // 2, SSM_STATE, LANES), F32),
            pltpu.VMEM((q_len, SSM_D_INNER), F32),
        ],
        compiler_params=_cparams(("arbitrary", "arbitrary"), 32 << 20),
        name="mamba2_ssd",
    )(z2, xbc2, dt2, dt_t, conv_w, conv_b.reshape(1, SSM_XBC_W),
      dt_bias.reshape(1, SSM_HEADS), dt_bias.reshape(SSM_HEADS, 1),
      a_log.reshape(1, SSM_HEADS), a_log.reshape(SSM_HEADS, 1),
      dsk, norm_w.reshape(1, SSM_D_INNER), expand)


def _ret_kernel(q_ref, k_ref, v_ref, g_ref, dm_ref, kd_ref, qd_ref, cd_ref, o_ref, rstate):
    @pl.when(pl.program_id(1) == 0)
    def _():
        rstate[...] = jnp.zeros_like(rstate)

    for h in range(RET_HEADS):
        ks = slice(h * RET_QK_DIM, (h + 1) * RET_QK_DIM)
        vs = slice(h * RET_V_DIM, (h + 1) * RET_V_DIM)
        qh = q_ref[:, ks]
        kf = k_ref[:, ks].astype(F32) * (RET_QK_DIM ** -0.5)
        vh = v_ref[:, vs]
        sc = lax.dot_general(qh, kf.astype(BF), (((1,), (1,)), ((), ())), preferred_element_type=F32)
        sc = sc * dm_ref[h]
        inner = jnp.dot(sc.astype(BF), vh, preferred_element_type=F32)
        r = rstate[h]
        cross = jnp.dot(qh, r.astype(BF), preferred_element_type=F32) * qd_ref[:, vs]
        y = inner + cross
        k2t = (kf * kd_ref[:, ks]).T.astype(BF)
        rstate[h] = r * cd_ref[:, vs] + jnp.dot(k2t, vh, preferred_element_type=F32)
        mu = jnp.mean(y, axis=-1, keepdims=True)
        dlt = y - mu
        var = jnp.mean(dlt * dlt, axis=-1, keepdims=True)
        gf = g_ref[:, vs].astype(F32)
        o_ref[:, vs] = (gf * _sigmoid(gf) * (dlt * lax.rsqrt(var + LN_EPS))).astype(o_ref.dtype)


def _retention(ret2, nb, seq):
    t = nb * seq
    c_len = RET_CHUNK
    nchunk = seq // c_len
    qk_w = RET_HEADS * RET_QK_DIM
    v_w = RET_HEADS * RET_V_DIM
    log_gamma = jnp.log1p(-jnp.exp2(-5.0 - jnp.arange(RET_HEADS, dtype=F32)))
    pos = jnp.arange(c_len, dtype=F32)
    diff = pos[:, None] - pos[None, :]
    dmask = jnp.where(diff >= 0, jnp.exp(log_gamma[:, None, None] * jnp.maximum(diff, 0.0)), 0.0)
    k_decay = jnp.exp((c_len - 1.0 - pos)[:, None] * log_gamma[None, :])
    q_decay = jnp.exp((pos + 1.0)[:, None] * log_gamma[None, :])
    chunk_decay = jnp.exp(log_gamma * c_len)
    kd = jnp.repeat(k_decay, RET_QK_DIM, axis=1)
    qd = jnp.repeat(q_decay, RET_V_DIM, axis=1)
    cd = jnp.repeat(chunk_decay, RET_V_DIM).reshape(1, v_w)
    blk = lambda w, off: pl.BlockSpec((c_len, w), lambda b, c: (b * nchunk + c, off))
    return pl.pallas_call(
        _ret_kernel,
        grid=(nb, nchunk),
        in_specs=[
            blk(qk_w, 0), blk(qk_w, 1), blk(v_w, 1), blk(v_w, 2),
            pl.BlockSpec((RET_HEADS, c_len, c_len), lambda b, c: (0, 0, 0)),
            pl.BlockSpec((c_len, qk_w), lambda b, c: (0, 0)),
            pl.BlockSpec((c_len, v_w), lambda b, c: (0, 0)),
            pl.BlockSpec((1, v_w), lambda b, c: (0, 0)),
        ],
        out_specs=pl.BlockSpec((c_len, v_w), lambda b, c: (b * nchunk + c, 0)),
        out_shape=jax.ShapeDtypeStruct((t, v_w), BF),
        scratch_shapes=[pltpu.VMEM((RET_HEADS, RET_QK_DIM, RET_V_DIM), F32)],
        compiler_params=_cparams(("arbitrary", "arbitrary"), 24 << 20),
        name="retention",
    )(ret2, ret2, ret2, ret2, dmask, kd, qd, cd)


def _merge_kernel(u_ref, ya_ref, ys_ref, yr_ref, wg0_ref, wg1_ref, wg2_ref, wb_ref, o_ref, wbs):
    @pl.when(pl.program_id(1) == 0)
    def _():
        wbs[...] = wb_ref[...].astype(BF)

    u = u_ref[...]
    acc = None
    for r, (y_ref, wg_ref) in enumerate(((ya_ref, wg0_ref), (ys_ref, wg1_ref), (yr_ref, wg2_ref))):
        gate = jnp.dot(u, wg_ref[...], preferred_element_type=F32)
        wide = jnp.dot(y_ref[...], wbs[r], preferred_element_type=F32)
        term = _sigmoid(gate) * wide
        acc = term if acc is None else acc + term
    o_ref[...] = acc.astype(o_ref.dtype)


def _branch_merge(u, y_attn, y_ssm, y_ret, w_tail, w_branch, layer):
    t, d = u.shape
    tm, tn = 512, 256
    goff = OFF_GATES_IN_TAIL // tn
    per = D_MODEL // tn
    ub = pl.BlockSpec((tm, d), lambda j, i: (i, 0))
    yb = pl.BlockSpec((tm, BRANCH_WIDTH), lambda j, i: (i, 0))
    wg = lambda r: pl.BlockSpec((d, tn), lambda j, i: (0, goff + r * per + j))
    vmem = 2 * (tm * d * 2 + 3 * tm * BRANCH_WIDTH * 2 + 3 * d * tn * 2 + 3 * BRANCH_WIDTH * tn * 4)
    vmem += 3 * BRANCH_WIDTH * tn * 2 + 8 * tm * tn * 4
    return pl.pallas_call(
        _merge_kernel,
        grid=(D_MODEL // tn, t // tm),
        in_specs=[ub, yb, yb, yb, wg(0), wg(1), wg(2),
                  pl.BlockSpec((None, N_BRANCH, BRANCH_WIDTH, tn), lambda j, i: (layer, 0, 0, j))],
        out_specs=pl.BlockSpec((tm, tn), lambda j, i: (i, j)),
        out_shape=jax.ShapeDtypeStruct((t, D_MODEL), BF),
        scratch_shapes=[pltpu.VMEM((N_BRANCH, BRANCH_WIDTH, tn), BF)],
        compiler_params=_cparams(("arbitrary", "arbitrary"), vmem),
        name="branch_merge",
    )(u, y_attn, y_ssm, y_ret, w_tail, w_tail, w_tail, w_branch)


def _swiglu_kernel(u_ref, w1_ref, w3_ref, *rest, scaled):
    if scaled:
        comb_ref, o_ref, w1s, w3s = rest
    else:
        o_ref, w1s, w3s = rest

    @pl.when(pl.program_id(2) == 0)
    def _():
        w1s[...] = w1_ref[...].astype(BF)
        w3s[...] = w3_ref[...].astype(BF)

    u = u_ref[...]
    a = jnp.dot(u, w1s[...], preferred_element_type=F32)
    b = jnp.dot(u, w3s[...], preferred_element_type=F32)
    h = a * _sigmoid(a) * b
    if scaled:
        comb = comb_ref[...]
        lane = lax.broadcasted_iota(jnp.int32, comb.shape, 1)
        h = h * jnp.sum(jnp.where(lane == pl.program_id(0), comb, 0.0), axis=-1, keepdims=True)
    o_ref[...] = h.astype(o_ref.dtype)


def _swiglu_hidden(u, w1, w3, lead, n_exp, comb, tm, tn, name):
    t, d = u.shape
    f = w1.shape[-1]
    per = f // tn
    scaled = comb is not None
    w_spec = pl.BlockSpec((None,) * (len(lead) + 1) + (d, tn), lambda e, j, i: tuple(lead) + (e, 0, j))
    in_specs = [pl.BlockSpec((tm, d), lambda e, j, i: (i, 0)), w_spec, w_spec]
    args = [u, w1, w3]
    if scaled:
        in_specs.append(pl.BlockSpec((tm, LANES), lambda e, j, i: (i, 0)))
        args.append(comb)
    vmem = 2 * tm * d * 2 + 4 * d * tn * 4 + 2 * d * tn * 2 + 2 * tm * tn * 2 + 4 * tm * tn * 4
    return pl.pallas_call(
        functools.partial(_swiglu_kernel, scaled=scaled),
        grid=(n_exp, per, t // tm),
        in_specs=in_specs,
        out_specs=pl.BlockSpec((tm, tn), lambda e, j, i: (i, e * per + j)),
        out_shape=jax.ShapeDtypeStruct((t, n_exp * f), BF),
        scratch_shapes=[pltpu.VMEM((d, tn), BF), pltpu.VMEM((d, tn), BF)],
        compiler_params=_cparams(("arbitrary", "arbitrary", "arbitrary"), vmem),
        name=name,
    )(*args)


def _router_kernel(x_ref, sc_ref, sh_ref, rw_ref, o_ref):
    u = x_ref[...] * (1.0 + sc_ref[...]) + sh_ref[...]
    logits = jnp.dot(u, rw_ref[...], precision=HIGHEST, preferred_element_type=F32)
    lane = lax.broadcasted_iota(jnp.int32, logits.shape, 1)
    lg = jnp.where(lane < N_EXPERTS, logits, -jnp.inf)
    m1 = jnp.max(lg, axis=-1, keepdims=True)
    i1 = jnp.min(jnp.where(lg == m1, lane, LANES), axis=-1, keepdims=True)
    lg2 = jnp.where(lane == i1, -jnp.inf, lg)
    m2 = jnp.max(lg2, axis=-1, keepdims=True)
    i2 = jnp.min(jnp.where(lg2 == m2, lane, LANES), axis=-1, keepdims=True)
    e2 = jnp.exp(m2 - m1)
    inv = 1.0 / (1.0 + e2)
    o_ref[...] = jnp.where(lane == i1, inv, 0.0) + jnp.where(lane == i2, e2 * inv, 0.0)


def _router(x2, sc, sh, router_w, seq):
    t, d = x2.shape
    tm = 256
    per_b = seq // tm
    rw = jnp.pad(router_w, ((0, 0), (0, LANES - N_EXPERTS)))
    vec = pl.BlockSpec((None, 1, d), lambda i: (i // per_b, 0, 0))
    return pl.pallas_call(
        _router_kernel,
        grid=(t // tm,),
        in_specs=[pl.BlockSpec((tm, d), lambda i: (i, 0)), vec, vec,
                  pl.BlockSpec((d, LANES), lambda i: (0, 0))],
        out_specs=pl.BlockSpec((tm, LANES), lambda i: (i, 0)),
        out_shape=jax.ShapeDtypeStruct((t, LANES), F32),
        compiler_params=_cparams(("arbitrary",), 2 * tm * d * 4 + 2 * d * LANES * 4 + 4 * tm * d * 4),
        name="moe_router",
    )(x2, sc, sh, rw)


def kernel(x, c, w_ada, b_ada, w_in, attn_sinks, conv_w, conv_b, dt_bias, a_log, d_skip, ssm_norm_w,
           w_branch, w_o, ln_g, ln_b, ffn_w1, ffn_w3, ffn_w2, router_w, exp_w1, exp_w3, exp_w2):
    nb, seq, d = x.shape
    t = nb * seq
    x2 = x.reshape(t, d)
    mod = _adaln_mod(c, w_ada, b_ada).reshape(DEPTH, nb, 6, 1, d)
    u = _modulate(x2, mod[0, :, 1], mod[0, :, 0], seq)
    for l in range(DEPTH):
        sh1, sc1, g1, sh2, sc2, g2 = (mod[l, :, i] for i in range(6))
        w_tail = w_in[l, :, OFF_TAIL:].astype(BF)
        q2 = _matmul_ws(u, w_in, (l,), OFF_Q, OFF_KV - OFF_Q, BF, 1024, 512, "proj_q")
        kv2 = _matmul_ws(u, w_in, (l,), OFF_KV, OFF_Z - OFF_KV, BF, 1024, 512, "proj_kv")
        z2 = _matmul_ws(u, w_in, (l,), OFF_Z, OFF_XBC - OFF_Z, BF, 1024, 512, "proj_z")
        xbc2 = _matmul_ws(u, w_in, (l,), OFF_XBC, OFF_DT - OFF_XBC, BF, 1024, 512, "proj_xbc")
        dt2 = _matmul_ws(u, w_in, (l,), OFF_DT, LANES, F32, 1024, LANES, "proj_dt")[:, :SSM_HEADS]
        ret2 = _matmul_ws(u, w_tail, (), 0, RET_WIDTH, BF, 1024, 512, "proj_ret")
        y_attn = _attention(q2, kv2, attn_sinks[l], nb, seq)
        y_ssm = _ssd(z2, xbc2, dt2, conv_w[l], conv_b[l], dt_bias[l], a_log[l], d_skip[l],
                     ssm_norm_w[l], nb, seq)
        y_ret = _retention(ret2, nb, seq)
        merged = _branch_merge(u, y_attn, y_ssm, y_ret, w_tail, w_branch, l)
        mix = _matmul_ws(merged, w_o, (l,), 0, d, F32, 1024, 512, "proj_o")
        x2, u = _residual_ln(x2, mix, g1, ln_g[l, 0], ln_b[l, 0], seq, sc2, sh2)
        if l % 2 == 0:
            hid = _swiglu_hidden(u, ffn_w1, ffn_w3, (), 1, None, 1024, 256, "ffn_hidden")
            f = _matmul_k(hid, ffn_w2, (l // 2,), F32, 2048, 1024, 256, "ffn_out")
        else:
            comb = _router(x2, sc2, sh2, router_w[l // 2], seq)
            hid = _swiglu_hidden(u, exp_w1, exp_w3, (l // 2,), N_EXPERTS, comb, 1024, 256, "moe_hidden")
            w2 = exp_w2[l // 2].reshape(N_EXPERTS * D_FF_EXPERT, d)
            f = _matmul_k(hid, w2, (), F32, 2048, 1024, 512, "moe_out")
        if l + 1 < DEPTH:
            x2, u = _residual_ln(x2, f, g2, ln_g[l, 1], ln_b[l, 1], seq, mod[l + 1, :, 1], mod[l + 1, :, 0])
        else:
            x2, _ = _residual_ln(x2, f, g2, ln_g[l, 1], ln_b[l, 1], seq)
    return x2.reshape(nb, seq, d)
```

```python
import functools
import math

import jax
import jax.numpy as jnp
from jax import lax
from jax.experimental import pallas as pl
from jax.experimental.pallas import tpu as pltpu

F32 = jnp.float32
BF = jnp.bfloat16
HIGHEST = lax.Precision.HIGHEST

D_MODEL = 4096
DEPTH = 2
ATTN_HEADS = 32
ATTN_KV_HEADS = 4
ATTN_GROUP = ATTN_HEADS // ATTN_KV_HEADS
ATTN_HEAD_DIM = 64
WINDOW = 128
ATTN_SCALE = ATTN_HEAD_DIM ** -0.5
SSM_D_INNER = 2048
SSM_HEAD_DIM = 64
SSM_HEADS = SSM_D_INNER // SSM_HEAD_DIM
SSM_GROUPS = 4
SSM_STATE = 128
SSM_CONV = 4
SSM_CHUNK = 128
SSM_XBC_W = SSM_D_INNER + 2 * SSM_GROUPS * SSM_STATE
SSM_NORM_GROUP = SSM_D_INNER // SSM_GROUPS
RET_HEADS = 8
RET_QK_DIM = 128
RET_V_DIM = 256
RET_CHUNK = 128
N_BRANCH = 3
BRANCH_WIDTH = 2048
D_FF_DENSE = 11008
N_EXPERTS = 8
D_FF_EXPERT = 2048
DN_ALPHA = (2 * DEPTH) ** 0.25
LN_EPS = 1e-5
RMS_EPS = 1e-5

OFF_Q = 0
OFF_KV = OFF_Q + ATTN_HEADS * ATTN_HEAD_DIM
OFF_Z = OFF_KV + 2 * ATTN_KV_HEADS * ATTN_HEAD_DIM
OFF_XBC = OFF_Z + SSM_D_INNER
OFF_DT = OFF_XBC + SSM_XBC_W
OFF_TAIL = OFF_DT + SSM_HEADS
RET_WIDTH = 2 * RET_HEADS * RET_QK_DIM + 2 * RET_HEADS * RET_V_DIM
OFF_GATES_IN_TAIL = RET_WIDTH

LANES = 128
SUBLANES = 8
VMEM_LIMIT_CAP = 56 << 20
MASK_VALUE = -1e30


def _cparams(semantics, vmem_bytes):
    limit = int(min(max(vmem_bytes + (8 << 20), 24 << 20), VMEM_LIMIT_CAP))
    return pltpu.CompilerParams(dimension_semantics=semantics, vmem_limit_bytes=limit)


def _sigmoid(v):
    return 1.0 / (1.0 + jnp.exp(-v))


def _softplus(v):
    return jnp.maximum(v, 0.0) + jnp.log(1.0 + jnp.exp(-jnp.abs(v)))


def _mod_kernel(c_ref, w_ref, b_ref, o_ref):
    c = c_ref[...]
    cond = c * _sigmoid(c)
    o_ref[...] = jnp.dot(cond.astype(BF), w_ref[...].astype(BF),
                         preferred_element_type=F32) + b_ref[...]


def _adaln_mod(c, w_ada, b_ada):
    nb = c.shape[0]
    depth, d, n = w_ada.shape
    tn = 512
    c8 = jnp.pad(c, ((0, SUBLANES - nb), (0, 0)))
    out = pl.pallas_call(
        _mod_kernel,
        grid=(depth, n // tn),
        in_specs=[
            pl.BlockSpec((SUBLANES, d), lambda l, j: (0, 0)),
            pl.BlockSpec((None, d, tn), lambda l, j: (l, 0, j)),
            pl.BlockSpec((None, 1, tn), lambda l, j: (l, 0, j)),
        ],
        out_specs=pl.BlockSpec((None, SUBLANES, tn), lambda l, j: (l, 0, j)),
        out_shape=jax.ShapeDtypeStruct((depth, SUBLANES, n), F32),
        compiler_params=_cparams(("arbitrary", "arbitrary"), 2 * d * tn * 4 + d * tn * 2),
        name="adaln_mod",
    )(c8, w_ada, b_ada.reshape(depth, 1, n))
    return out[:, :nb, :]


def _modulate_kernel(x_ref, sc_ref, sh_ref, o_ref):
    o_ref[...] = (x_ref[...] * (1.0 + sc_ref[...]) + sh_ref[...]).astype(BF)


def _modulate(x2, sc, sh, seq):
    t, d = x2.shape
    tm = 512
    per_b = seq // tm
    vec = pl.BlockSpec((None, 1, d), lambda i: (i // per_b, 0, 0))
    return pl.pallas_call(
        _modulate_kernel,
        grid=(t // tm,),
        in_specs=[pl.BlockSpec((tm, d), lambda i: (i, 0)), vec, vec],
        out_specs=pl.BlockSpec((tm, d), lambda i: (i, 0)),
        out_shape=jax.ShapeDtypeStruct((t, d), BF),
        compiler_params=_cparams(("arbitrary",), 2 * tm * d * 6),
        name="modulate",
    )(x2, sc, sh)


def _realign_kernel(a_ref, b_ref, o_ref, *, shift):
    tk, tn = o_ref.shape
    step = 256
    for r0 in range(0, tk, step):
        rows = slice(r0, r0 + step)
        full = jnp.concatenate([a_ref[rows, :], b_ref[rows, :]], axis=1)
        o_ref[rows, :] = full[:, shift:shift + tn].astype(BF)


def _realign_tail(w_in, layer):
    _, k, n_all = w_in.shape
    n = n_all - OFF_TAIL
    shift = OFF_TAIL % LANES
    base = OFF_TAIL - shift
    tk, tn = 1024, 1536
    assert shift > 0 and base % tn == 0 and n % tn == 0 and k % tk == 0
    return pl.pallas_call(
        functools.partial(_realign_kernel, shift=shift),
        grid=(k // tk, n // tn),
        in_specs=[
            pl.BlockSpec((None, tk, tn), lambda i, j: (layer, i, base // tn + j)),
            pl.BlockSpec((None, tk, LANES), lambda i, j: (layer, i, (base + (j + 1) * tn) // LANES)),
        ],
        out_specs=pl.BlockSpec((tk, tn), lambda i, j: (i, j)),
        out_shape=jax.ShapeDtypeStruct((k, n), BF),
        compiler_params=_cparams(("arbitrary", "arbitrary"), 2 * tk * (tn + LANES) * 4 + 2 * tk * tn * 2 + (4 << 20)),
        name="realign_tail",
    )(w_in, w_in)


def _ln_tail(r, g_ref, b_ref, rest, emit_u):
    mu = jnp.mean(r, axis=-1, keepdims=True)
    dlt = r - mu
    var = jnp.mean(dlt * dlt, axis=-1, keepdims=True)
    xn = dlt * lax.rsqrt(var + LN_EPS) * g_ref[...] + b_ref[...]
    if emit_u:
        sc_ref, sh_ref, xo_ref, uo_ref = rest
        xo_ref[...] = xn
        uo_ref[...] = (xn * (1.0 + sc_ref[...]) + sh_ref[...]).astype(uo_ref.dtype)
    else:
        (xo_ref,) = rest
        xo_ref[...] = xn


def _ln_kernel(x_ref, y_ref, gate_ref, g_ref, b_ref, *rest, emit_u):
    r = DN_ALPHA * x_ref[...] + gate_ref[...] * y_ref[...]
    _ln_tail(r, g_ref, b_ref, rest, emit_u)


def _residual_ln(x2, y2, gate, ln_g, ln_b, seq, sc=None, sh=None, u_dtype=BF):
    t, d = x2.shape
    tm = 256
    per_b = seq // tm
    emit_u = sc is not None
    row = pl.BlockSpec((tm, d), lambda i: (i, 0))
    vec = pl.BlockSpec((None, 1, d), lambda i: (i // per_b, 0, 0))
    par = pl.BlockSpec((1, d), lambda i: (0, 0))
    in_specs = [row, row, vec, par, par]
    args = [x2, y2, gate, ln_g.reshape(1, d), ln_b.reshape(1, d)]
    out_specs = [row]
    out_shape = [jax.ShapeDtypeStruct((t, d), F32)]
    if emit_u:
        in_specs += [vec, vec]
        args += [sc, sh]
        out_specs.append(row)
        out_shape.append(jax.ShapeDtypeStruct((t, d), u_dtype))
    outs = pl.pallas_call(
        functools.partial(_ln_kernel, emit_u=emit_u),
        grid=(t // tm,),
        in_specs=in_specs,
        out_specs=out_specs,
        out_shape=out_shape,
        compiler_params=_cparams(("arbitrary",), 2 * tm * d * 14 + 4 * tm * d * 4),
        name="residual_ln",
    )(*args)
    return outs if emit_u else (outs[0], None)


def _mm_ws_kernel(a_ref, b_ref, o_ref, *scratch):
    if scratch:
        (wbf,) = scratch

        @pl.when(pl.program_id(1) == 0)
        def _():
            wbf[...] = b_ref[...].astype(BF)

        w = wbf[...]
    else:
        w = b_ref[...]
    o_ref[...] = jnp.dot(a_ref[...], w, preferred_element_type=F32).astype(o_ref.dtype)


def _matmul_ws(a, w, lead, col0, n, out_dtype, tm, tn, name):
    m, k = a.shape
    joff = col0 // tn
    assert col0 % tn == 0 and n % tn == 0 and m % tm == 0
    nlead = len(lead)
    w_spec = pl.BlockSpec((None,) * nlead + (k, tn), lambda j, i: tuple(lead) + (0, j + joff))
    cast = w.dtype != BF
    scratch = [pltpu.VMEM((k, tn), BF)] if cast else []
    osz = jnp.dtype(out_dtype).itemsize
    vmem = 2 * tm * k * 2 + 2 * k * tn * w.dtype.itemsize + k * tn * 2 * cast + 2 * tm * tn * osz + tm * tn * 4
    return pl.pallas_call(
        _mm_ws_kernel,
        grid=(n // tn, m // tm),
        in_specs=[pl.BlockSpec((tm, k), lambda j, i: (i, 0)), w_spec],
        out_specs=pl.BlockSpec((tm, tn), lambda j, i: (i, j)),
        out_shape=jax.ShapeDtypeStruct((m, n), out_dtype),
        scratch_shapes=scratch,
        compiler_params=_cparams(("arbitrary", "arbitrary"), vmem),
        name=name,
    )(a, w)


def _mm_k_kernel(a_ref, b_ref, o_ref, acc_ref, *, k_valid):
    kk = pl.program_id(2)
    tk = b_ref.shape[0]

    @pl.when(kk == 0)
    def _():
        acc_ref[...] = jnp.zeros_like(acc_ref)

    b = b_ref[...]
    if k_valid % tk:
        row = lax.broadcasted_iota(jnp.int32, b.shape, 0) + kk * tk
        b = jnp.where(row < k_valid, b, 0.0)
    acc_ref[...] += jnp.dot(a_ref[...], b.astype(BF), preferred_element_type=F32)

    @pl.when(kk == pl.num_programs(2) - 1)
    def _():
        o_ref[...] = acc_ref[...].astype(o_ref.dtype)


def _matmul_k(a, w, lead, out_dtype, tm, tn, tk, name):
    m, k = a.shape
    k_valid, n = w.shape[-2:]
    assert m % tm == 0 and n % tn == 0 and k % tk == 0 and k - tk < k_valid <= k
    nlead = len(lead)
    w_spec = pl.BlockSpec((None,) * nlead + (tk, tn), lambda i, j, kk: tuple(lead) + (kk, j))
    osz = jnp.dtype(out_dtype).itemsize
    vmem = 2 * tm * tk * 2 + 2 * tk * tn * 4 + tk * tn * 2 + 2 * tm * tn * osz + 2 * tm * tn * 4
    return pl.pallas_call(
        functools.partial(_mm_k_kernel, k_valid=k_valid),
        grid=(m // tm, n // tn, k // tk),
        in_specs=[pl.BlockSpec((tm, tk), lambda i, j, kk: (i, kk)), w_spec],
        out_specs=pl.BlockSpec((tm, tn), lambda i, j, kk: (i, j)),
        out_shape=jax.ShapeDtypeStruct((m, n), out_dtype),
        scratch_shapes=[pltpu.VMEM((tm, tn), F32)],
        compiler_params=_cparams(("arbitrary", "arbitrary", "arbitrary"), vmem),
        name=name,
    )(a, w)


def _attn_kernel(q_ref, kvp_ref, kvc_ref, bp_ref, bc_ref, sink_ref, o_ref):
    nblk = pl.program_id(1)
    blk = q_ref.shape[0]
    dh = ATTN_HEAD_DIM
    kvw = ATTN_KV_HEADS * dh
    q = q_ref[...]
    kvp = kvp_ref[...]
    kvc = kvc_ref[...]
    dims = (((1,), (1,)), ((), ()))
    outs = []
    for kh in range(ATTN_KV_HEADS):
        heads = range(kh * ATTN_GROUP, (kh + 1) * ATTN_GROUP)
        qs = jnp.concatenate([q[:, h * dh:(h + 1) * dh] for h in heads], axis=0)
        kcol = slice(kh * dh, (kh + 1) * dh)
        vcol = slice(kvw + kh * dh, kvw + (kh + 1) * dh)
        sp = lax.dot_general(qs, kvp[:, kcol], dims, preferred_element_type=F32).reshape(ATTN_GROUP, blk, blk)
        sc = lax.dot_general(qs, kvc[:, kcol], dims, preferred_element_type=F32).reshape(ATTN_GROUP, blk, blk)
        sp = sp * ATTN_SCALE + bp_ref[kh]
        sp = jnp.where(nblk == 0, MASK_VALUE, sp)
        sc = sc * ATTN_SCALE + bc_ref[kh]
        sink = sink_ref[kh][:, :, :1]
        m = jnp.maximum(jnp.maximum(jnp.max(sp, axis=-1, keepdims=True),
                                    jnp.max(sc, axis=-1, keepdims=True)), sink)
        pp = jnp.exp(sp - m)
        pc = jnp.exp(sc - m)
        den = jnp.sum(pp, axis=-1, keepdims=True) + jnp.sum(pc, axis=-1, keepdims=True) + jnp.exp(sink - m)
        o = jnp.dot(pp.astype(BF).reshape(ATTN_GROUP * blk, blk), kvp[:, vcol], preferred_element_type=F32)
        o = o + jnp.dot(pc.astype(BF).reshape(ATTN_GROUP * blk, blk), kvc[:, vcol], preferred_element_type=F32)
        o = o.reshape(ATTN_GROUP, blk, dh) / den
        outs.extend(o[g] for g in range(ATTN_GROUP))
    o_ref[...] = jnp.concatenate(outs, axis=1).astype(o_ref.dtype)


def _attention(q2, kv2, sinks, nb, seq):
    blk = WINDOW
    nblk = seq // blk
    qw = ATTN_HEADS * ATTN_HEAD_DIM
    kvw2 = 2 * ATTN_KV_HEADS * ATTN_HEAD_DIM
    slopes = jnp.exp2(-8.0 * jnp.arange(1, ATTN_HEADS + 1, dtype=F32) / ATTN_HEADS)
    slopes = slopes.reshape(ATTN_KV_HEADS, ATTN_GROUP, 1, 1)
    qi = jnp.arange(blk)[:, None]
    kj = jnp.arange(blk)[None, :]
    dist_p = blk + qi - kj
    dist_c = qi - kj
    bias_p = jnp.where(dist_p < WINDOW, -slopes * dist_p.astype(F32), MASK_VALUE)
    bias_c = jnp.where(dist_c >= 0, -slopes * dist_c.astype(F32), MASK_VALUE)
    sink_t = jnp.broadcast_to(sinks.astype(F32).reshape(ATTN_KV_HEADS, ATTN_GROUP, 1, 1),
                              (ATTN_KV_HEADS, ATTN_GROUP, 1, LANES))
    row = lambda w: pl.BlockSpec((blk, w), lambda b, n: (b * nblk + n, 0))
    prev = pl.BlockSpec((blk, kvw2), lambda b, n: (b * nblk + jnp.maximum(n - 1, 0), 0))
    tab = pl.BlockSpec((ATTN_KV_HEADS, ATTN_GROUP, blk, blk), lambda b, n: (0, 0, 0, 0))
    sk = pl.BlockSpec((ATTN_KV_HEADS, ATTN_GROUP, 1, LANES), lambda b, n: (0, 0, 0, 0))
    return pl.pallas_call(
        _attn_kernel,
        grid=(nb, nblk),
        in_specs=[row(qw), prev, row(kvw2), tab, tab, sk],
        out_specs=row(qw),
        out_shape=jax.ShapeDtypeStruct((nb * seq, qw), BF),
        compiler_params=_cparams(("arbitrary", "arbitrary"), 24 << 20),
        name="swa_attention",
    )(q2, kv2, kv2, bias_p, bias_c, sink_t)


def _ssd_kernel(z_ref, x_ref, dt_ref, dtt_ref, cw_ref, cb_ref, dtb_ref, dtbt_ref, al_ref, alt_ref,
                dsk_ref, nw_ref, e_ref, o_ref, xbuf, state, yscr):
    q_len = SSM_CHUNK
    halo = SUBLANES

    @pl.when(pl.program_id(1) == 0)
    def _():
        xbuf[0:halo, :] = jnp.zeros((halo, SSM_XBC_W), F32)
        state[...] = jnp.zeros_like(state)

    xbuf[halo:halo + q_len, :] = x_ref[...].astype(F32)
    acc = cb_ref[...] + cw_ref[0:1, :] * xbuf[halo - 3:halo - 3 + q_len, :]
    for j in range(1, SSM_CONV):
        acc = acc + cw_ref[j:j + 1, :] * xbuf[halo - 3 + j:halo - 3 + j + q_len, :]
    xbuf[0:halo, :] = xbuf[q_len:q_len + halo, :]
    xbc = acc * _sigmoid(acc)
    xs = xbc[:, :SSM_D_INNER]
    bm = xbc[:, SSM_D_INNER:SSM_D_INNER + SSM_GROUPS * SSM_STATE]
    cm = xbc[:, SSM_D_INNER + SSM_GROUPS * SSM_STATE:]

    dt = _softplus(dt_ref[...] + dtb_ref[...])
    la = dt * (-jnp.exp(al_ref[...]))
    la_t = _softplus(dtt_ref[...] + dtbt_ref[...]) * (-jnp.exp(alt_ref[...]))
    row = lax.broadcasted_iota(jnp.int32, (q_len, q_len), 0)
    col = lax.broadcasted_iota(jnp.int32, (q_len, q_len), 1)
    causal = col <= row
    cs = jnp.dot(causal.astype(F32), la, precision=HIGHEST, preferred_element_type=F32)
    cs_t = jnp.dot(la_t, (row <= col).astype(F32), precision=HIGHEST, preferred_element_type=F32)
    ecs = jnp.exp(cs)
    dte = jnp.exp(cs[q_len - 1:q_len, :] - cs)
    stack = jnp.concatenate([dt, dt * dte, ecs], axis=0)
    wide = jnp.dot(stack, e_ref[...], precision=HIGHEST, preferred_element_type=F32)
    w_dt = wide[0:q_len]
    w_dd = wide[q_len:2 * q_len]
    w_e = wide[2 * q_len:3 * q_len]
    x_dt = (xs * w_dt).astype(BF)
    x_dd = (xs * w_dd).astype(BF)
    left = lax.broadcasted_iota(jnp.int32, (q_len, LANES), 1) < SSM_HEAD_DIM
    heads_per_group = SSM_HEADS // SSM_GROUPS
    for g in range(SSM_GROUPS):
        bg = bm[:, g * SSM_STATE:(g + 1) * SSM_STATE]
        cg = cm[:, g * SSM_STATE:(g + 1) * SSM_STATE].astype(BF)
        cbm = lax.dot_general(cg, bg.astype(BF), (((1,), (1,)), ((), ())), preferred_element_type=F32)
        bg_t = bg.T.astype(BF)
        for jp in range(heads_per_group // 2):
            p = g * (heads_per_group // 2) + jp
            h0 = 2 * p
            sl = slice(p * LANES, (p + 1) * LANES)
            l0 = jnp.exp(jnp.where(causal, cs[:, h0:h0 + 1] - cs_t[h0:h0 + 1, :], -jnp.inf))
            l1 = jnp.exp(jnp.where(causal, cs[:, h0 + 1:h0 + 2] - cs_t[h0 + 1:h0 + 2, :], -jnp.inf))
            xp = x_dt[:, sl]
            y0 = jnp.dot((cbm * l0).astype(BF), xp, preferred_element_type=F32)
            y1 = jnp.dot((cbm * l1).astype(BF), xp, preferred_element_type=F32)
            st = state[p]
            y_off = jnp.dot(cg, st.astype(BF), preferred_element_type=F32)
            yscr[:, sl] = jnp.where(left, y0, y1) + y_off * w_e[:, sl] + dsk_ref[:, sl] * xs[:, sl]
            state[p] = st * w_e[q_len - 1:q_len, sl] + jnp.dot(bg_t, x_dd[:, sl], preferred_element_type=F32)

    zf = z_ref[...].astype(F32)
    yg = yscr[...] * (zf * _sigmoid(zf))
    for g in range(SSM_GROUPS):
        gs = slice(g * SSM_NORM_GROUP, (g + 1) * SSM_NORM_GROUP)
        blk = yg[:, gs]
        ms = jnp.mean(blk * blk, axis=-1, keepdims=True)
        o_ref[:, gs] = (blk * lax.rsqrt(ms + RMS_EPS) * nw_ref[:, gs]).astype(o_ref.dtype)


def _ssd(z2, xbc2, dt2, conv_w, conv_b, dt_bias, a_log, d_skip, norm_w, nb, seq):
    t = nb * seq
    q_len = SSM_CHUNK
    nchunk = seq // q_len
    dt_t = dt2.reshape(nb, seq, SSM_HEADS).transpose(0, 2, 1)
    expand = jnp.repeat(jnp.eye(SSM_HEADS, dtype=F32), SSM_HEAD_DIM, axis=1)
    dsk = jnp.repeat(d_skip.astype(F32), SSM_HEAD_DIM).reshape(1, SSM_D_INNER)
    rowblk = lambda w: pl.BlockSpec((q_len, w), lambda b, c: (b * nchunk + c, 0))
    full = lambda r, w: pl.BlockSpec((r, w), lambda b, c: (0, 0))
    return pl.pallas_call(
        _ssd_kernel,
        grid=(nb, nchunk),
        in_specs=[
            rowblk(SSM_D_INNER), rowblk(SSM_XBC_W), rowblk(SSM_HEADS),
            pl.BlockSpec((None, SSM_HEADS, q_len), lambda b, c: (b, 0, c)),
            full(SSM_CONV, SSM_XBC_W), full(1, SSM_XBC_W),
            full(1, SSM_HEADS), full(SSM_HEADS, 1), full(1, SSM_HEADS), full(SSM_HEADS, 1),
            full(1, SSM_D_INNER), full(1, SSM_D_INNER), full(SSM_HEADS, SSM_D_INNER),
        ],
        out_specs=rowblk(SSM_D_INNER),
        out_shape=jax.ShapeDtypeStruct((t, SSM_D_INNER), BF),
        scratch_shapes=[
            pltpu.VMEM((q_len + 2 * SUBLANES, SSM_XBC_W), F32),
            pltpu.VMEM((SSM_HEADS // 2, SSM_STATE, LANES), F32),
            pltpu.VMEM((q_len, SSM_D_INNER), F32),
        ],
        compiler_params=_cparams(("arbitrary", "arbitrary"), 32 << 20),
        name="mamba2_ssd",
    )(z2, xbc2, dt2, dt_t, conv_w, conv_b.reshape(1, SSM_XBC_W),
      dt_bias.reshape(1, SSM_HEADS), dt_bias.reshape(SSM_HEADS, 1),
      a_log.reshape(1, SSM_HEADS), a_log.reshape(SSM_HEADS, 1),
      dsk, norm_w.reshape(1, SSM_D_INNER), expand)


def _ret_kernel(q_ref, k_ref, v_ref, g_ref, dm_ref, kd_ref, qd_ref, cd_ref, o_ref, rstate):
    @pl.when(pl.program_id(1) == 0)
    def _():
        rstate[...] = jnp.zeros_like(rstate)

    for h in range(RET_HEADS):
        ks = slice(h * RET_QK_DIM, (h + 1) * RET_QK_DIM)
        vs = slice(h * RET_V_DIM, (h + 1) * RET_V_DIM)
        qh = q_ref[:, ks]
        kf = k_ref[:, ks].astype(F32) * (RET_QK_DIM ** -0.5)
        vh = v_ref[:, vs]
        sc = lax.dot_general(qh, kf.astype(BF), (((1,), (1,)), ((), ())), preferred_element_type=F32)
        sc = sc * dm_ref[h]
        inner = jnp.dot(sc.astype(BF), vh, preferred_element_type=F32)
        r = rstate[h]
        cross = jnp.dot(qh, r.astype(BF), preferred_element_type=F32) * qd_ref[:, vs]
        y = inner + cross
        k2t = (kf * kd_ref[:, ks]).T.astype(BF)
        rstate[h] = r * cd_ref[:, vs] + jnp.dot(k2t, vh, preferred_element_type=F32)
        mu = jnp.mean(y, axis=-1, keepdims=True)
        dlt = y - mu
        var = jnp.mean(dlt * dlt, axis=-1, keepdims=True)
        gf = g_ref[:, vs].astype(F32)
        o_ref[:, vs] = (gf * _sigmoid(gf) * (dlt * lax.rsqrt(var + LN_EPS))).astype(o_ref.dtype)


def _retention(ret2, nb, seq):
    t = nb * seq
    c_len = RET_CHUNK
    nchunk = seq // c_len
    qk_w = RET_HEADS * RET_QK_DIM
    v_w = RET_HEADS * RET_V_DIM
    log_gamma = jnp.log1p(-jnp.exp2(-5.0 - jnp.arange(RET_HEADS, dtype=F32)))
    pos = jnp.arange(c_len, dtype=F32)
    diff = pos[:, None] - pos[None, :]
    dmask = jnp.where(diff >= 0, jnp.exp(log_gamma[:, None, None] * jnp.maximum(diff, 0.0)), 0.0)
    k_decay = jnp.exp((c_len - 1.0 - pos)[:, None] * log_gamma[None, :])
    q_decay = jnp.exp((pos + 1.0)[:, None] * log_gamma[None, :])
    chunk_decay = jnp.exp(log_gamma * c_len)
    kd = jnp.repeat(k_decay, RET_QK_DIM, axis=1)
    qd = jnp.repeat(q_decay, RET_V_DIM, axis=1)
    cd = jnp.repeat(chunk_decay, RET_V_DIM).reshape(1, v_w)
    blk = lambda w, off: pl.BlockSpec((c_len, w), lambda b, c: (b * nchunk + c, off))
    return pl.pallas_call(
        _ret_kernel,
        grid=(nb, nchunk),
        in_specs=[
            blk(qk_w, 0), blk(qk_w, 1), blk(v_w, 1), blk(v_w, 2),
            pl.BlockSpec((RET_HEADS, c_len, c_len), lambda b, c: (0, 0, 0)),
            pl.BlockSpec((c_len, qk_w), lambda b, c: (0, 0)),
            pl.BlockSpec((c_len, v_w), lambda b, c: (0, 0)),
            pl.BlockSpec((1, v_w), lambda b, c: (0, 0)),
        ],
        out_specs=pl.BlockSpec((c_len, v_w), lambda b, c: (b * nchunk + c, 0)),
        out_shape=jax.ShapeDtypeStruct((t, v_w), BF),
        scratch_shapes=[pltpu.VMEM((RET_HEADS, RET_QK_DIM, RET_V_DIM), F32)],
        compiler_params=_cparams(("arbitrary", "arbitrary"), 24 << 20),
        name="retention",
    )(ret2, ret2, ret2, ret2, dmask, kd, qd, cd)


def _merge_kernel(u_ref, ya_ref, ys_ref, yr_ref, wg0_ref, wg1_ref, wg2_ref, wb_ref, o_ref, wbs):
    @pl.when(pl.program_id(1) == 0)
    def _():
        wbs[...] = wb_ref[...].astype(BF)

    u = u_ref[...]
    acc = None
    for r, (y_ref, wg_ref) in enumerate(((ya_ref, wg0_ref), (ys_ref, wg1_ref), (yr_ref, wg2_ref))):
        gate = jnp.dot(u, wg_ref[...], preferred_element_type=F32)
        wide = jnp.dot(y_ref[...], wbs[r], preferred_element_type=F32)
        term = _sigmoid(gate) * wide
        acc = term if acc is None else acc + term
    o_ref[...] = acc.astype(o_ref.dtype)


def _branch_merge(u, y_attn, y_ssm, y_ret, w_tail, w_branch, layer):
    t, d = u.shape
    tm, tn = 512, 256
    goff = OFF_GATES_IN_TAIL // tn
    per = D_MODEL // tn
    ub = pl.BlockSpec((tm, d), lambda j, i: (i, 0))
    yb = pl.BlockSpec((tm, BRANCH_WIDTH), lambda j, i: (i, 0))
    wg = lambda r: pl.BlockSpec((d, tn), lambda j, i: (0, goff + r * per + j))
    vmem = 2 * (tm * d * 2 + 3 * tm * BRANCH_WIDTH * 2 + 3 * d * tn * 2 + 3 * BRANCH_WIDTH * tn * 4)
    vmem += 3 * BRANCH_WIDTH * tn * 2 + 8 * tm * tn * 4
    return pl.pallas_call(
        _merge_kernel,
        grid=(D_MODEL // tn, t // tm),
        in_specs=[ub, yb, yb, yb, wg(0), wg(1), wg(2),
                  pl.BlockSpec((None, N_BRANCH, BRANCH_WIDTH, tn), lambda j, i: (layer, 0, 0, j))],
        out_specs=pl.BlockSpec((tm, tn), lambda j, i: (i, j)),
        out_shape=jax.ShapeDtypeStruct((t, D_MODEL), BF),
        scratch_shapes=[pltpu.VMEM((N_BRANCH, BRANCH_WIDTH, tn), BF)],
        compiler_params=_cparams(("arbitrary", "arbitrary"), vmem),
        name="branch_merge",
    )(u, y_attn, y_ssm, y_ret, w_tail, w_tail, w_tail, w_branch)


def _swiglu_kernel(u_ref, w1_ref, w3_ref, o_ref, w1s, w3s, *, n_blocks):
    j = pl.program_id(0)

    @pl.when((pl.program_id(1) == 0) & (j < n_blocks))
    def _():
        w1s[...] = w1_ref[...].astype(BF)
        w3s[...] = w3_ref[...].astype(BF)

    @pl.when(j < n_blocks)
    def _():
        u = u_ref[...]
        a = jnp.dot(u, w1s[...], preferred_element_type=F32)
        b = jnp.dot(u, w3s[...], preferred_element_type=F32)
        o_ref[...] = (a * _sigmoid(a) * b).astype(o_ref.dtype)

    @pl.when(j >= n_blocks)
    def _():
        o_ref[...] = jnp.zeros_like(o_ref)


def _swiglu_hidden(u, w1, w3, lead, f_out, tm, tn, name):
    t, d = u.shape
    f = w1.shape[-1]
    assert f % tn == 0 and f_out % tn == 0 and f_out >= f
    n_blocks = f // tn
    w_spec = pl.BlockSpec((None, d, tn), lambda j, i: (lead, 0, jnp.minimum(j, n_blocks - 1)))
    vmem = 2 * tm * d * 2 + 4 * d * tn * 4 + 2 * d * tn * 2 + 2 * tm * tn * 2 + 4 * tm * tn * 4
    return pl.pallas_call(
        functools.partial(_swiglu_kernel, n_blocks=n_blocks),
        grid=(f_out // tn, t // tm),
        in_specs=[pl.BlockSpec((tm, d), lambda j, i: (i, 0)), w_spec, w_spec],
        out_specs=pl.BlockSpec((tm, tn), lambda j, i: (i, j)),
        out_shape=jax.ShapeDtypeStruct((t, f_out), BF),
        scratch_shapes=[pltpu.VMEM((d, tn), BF), pltpu.VMEM((d, tn), BF)],
        compiler_params=_cparams(("arbitrary", "arbitrary"), vmem),
        name=name,
    )(u, w1, w3)


MOE_TILE = 512
ROUTE_W1, ROUTE_W2, ROUTE_E1, ROUTE_E2 = 0, 1, 2, 3


def _router_kernel(u_ref, rw_ref, o_ref):
    logits = jnp.dot(u_ref[...], rw_ref[...], precision=HIGHEST, preferred_element_type=F32)
    lane = lax.broadcasted_iota(jnp.int32, logits.shape, 1)
    lg = jnp.where(lane < N_EXPERTS, logits, -jnp.inf)
    m1 = jnp.max(lg, axis=-1, keepdims=True)
    i1 = jnp.min(jnp.where(lg == m1, lane, LANES), axis=-1, keepdims=True)
    lg2 = jnp.where(lane == i1, -jnp.inf, lg)
    m2 = jnp.max(lg2, axis=-1, keepdims=True)
    i2 = jnp.min(jnp.where(lg2 == m2, lane, LANES), axis=-1, keepdims=True)
    e2 = jnp.exp(m2 - m1)
    inv = 1.0 / (1.0 + e2)
    rec = jnp.where(lane == ROUTE_W1, inv, 0.0)
    rec = jnp.where(lane == ROUTE_W2, e2 * inv, rec)
    rec = jnp.where(lane == ROUTE_E1, i1.astype(F32), rec)
    rec = jnp.where(lane == ROUTE_E2, i2.astype(F32), rec)
    o_ref[...] = rec


def _router(u32, router_w):
    t, d = u32.shape
    tm = 256
    rw = jnp.pad(router_w, ((0, 0), (0, LANES - N_EXPERTS)))
    return pl.pallas_call(
        _router_kernel,
        grid=(t // tm,),
        in_specs=[pl.BlockSpec((tm, d), lambda i: (i, 0)), pl.BlockSpec((d, LANES), lambda i: (0, 0))],
        out_specs=pl.BlockSpec((tm, LANES), lambda i: (i, 0)),
        out_shape=jax.ShapeDtypeStruct((t, LANES), F32),
        compiler_params=_cparams(("arbitrary",), 2 * tm * d * 4 + 2 * d * LANES * 4 + 4 * tm * d * 4),
        name="moe_router",
    )(u32, rw)


def _route_plan(route, tm):
    t = route.shape[0]
    n_rows = 2 * t + N_EXPERTS * tm
    e_flat = jnp.concatenate([route[:, ROUTE_E1], route[:, ROUTE_E2]]).astype(jnp.int32)
    onehot = (e_flat[:, None] == jnp.arange(N_EXPERTS, dtype=jnp.int32)[None, :]).astype(jnp.int32)
    csum = jnp.cumsum(onehot, axis=0)
    rank = jnp.sum(csum * onehot, axis=1) - 1
    counts = csum[-1]
    padded = ((counts + tm - 1) // tm) * tm
    p_end = jnp.cumsum(padded)
    pos = (p_end - padded)[e_flat] + rank
    src_tok = jnp.zeros((n_rows,), jnp.int32).at[pos].set(jnp.arange(2 * t, dtype=jnp.int32) % t)
    tile_start = jnp.arange(n_rows // tm, dtype=jnp.int32) * tm
    tile_expert = jnp.minimum(jnp.searchsorted(p_end, tile_start, side="right"), N_EXPERTS - 1).astype(jnp.int32)
    return src_tok, tile_expert, pos[:t], pos[t:]


def _row_gather_issue(idx_refs, src_hbm, buf, sem, tile, slot, rows):
    def body(r, carry):
        for which, idx_ref in enumerate(idx_refs):
            p = idx_ref[tile * rows + r]
            pltpu.make_async_copy(src_hbm.at[pl.ds(p, 1), :], buf.at[slot, which, pl.ds(r, 1), :],
                                  sem.at[slot]).start()
        return carry

    lax.fori_loop(0, rows, body, 0)


def _row_gather_step(idx_refs, src_hbm, buf, sem, rows):
    i = pl.program_id(0)

    @pl.when(i == 0)
    def _():
        _row_gather_issue(idx_refs, src_hbm, buf, sem, 0, 0, rows)

    @pl.when(i + 1 < pl.num_programs(0))
    def _():
        _row_gather_issue(idx_refs, src_hbm, buf, sem, i + 1, (i + 1) % 2, rows)

    slot = i % 2
    for which in range(len(idx_refs)):
        pltpu.make_async_copy(src_hbm.at[pl.ds(0, rows), :], buf.at[slot, which], sem.at[slot]).wait()
    return slot


def _gather_kernel(src_ref, u_hbm, o_ref, buf, sem):
    slot = _row_gather_step((src_ref,), u_hbm, buf, sem, o_ref.shape[0])
    o_ref[...] = buf[slot, 0].astype(o_ref.dtype)


def _gather_rows(src_tok, u32):
    n_rows = src_tok.shape[0]
    d = u32.shape[1]
    tg = 256
    return pl.pallas_call(
        _gather_kernel,
        grid_spec=pltpu.PrefetchScalarGridSpec(
            num_scalar_prefetch=1,
            grid=(n_rows // tg,),
            in_specs=[pl.BlockSpec(memory_space=pl.ANY)],
            out_specs=pl.BlockSpec((tg, d), lambda i, src: (i, 0)),
            scratch_shapes=[pltpu.VMEM((2, 1, tg, d), F32), pltpu.SemaphoreType.DMA((2,))],
        ),
        out_shape=jax.ShapeDtypeStruct((n_rows, d), BF),
        compiler_params=_cparams(("arbitrary",), 2 * tg * d * 4 + 2 * tg * d * 2 + tg * d * 4),
        name="moe_gather",
    )(src_tok, u32)


def _expert_changed(te_ref, tile):
    return (tile == 0) | (te_ref[tile] != te_ref[jnp.maximum(tile - 1, 0)])


def _moe_hidden_kernel(te_ref, us_ref, w1_ref, w3_ref, o_ref, w1s, w3s):
    @pl.when(_expert_changed(te_ref, pl.program_id(1)))
    def _():
        w1s[...] = w1_ref[...].astype(BF)
        w3s[...] = w3_ref[...].astype(BF)

    us = us_ref[...]
    a = jnp.dot(us, w1s[...], preferred_element_type=F32)
    b = jnp.dot(us, w3s[...], preferred_element_type=F32)
    o_ref[...] = (a * _sigmoid(a) * b).astype(o_ref.dtype)


def _moe_hidden(tile_expert, us, w1, w3, lead, tm):
    n_rows, d = us.shape
    f = w1.shape[-1]
    tn = 256
    w_spec = pl.BlockSpec((None, None, d, tn), lambda j, i, te: (lead, te[i], 0, j))
    vmem = 2 * tm * d * 2 + 4 * d * tn * 4 + 2 * d * tn * 2 + 2 * tm * tn * 2 + 4 * tm * tn * 4
    return pl.pallas_call(
        _moe_hidden_kernel,
        grid_spec=pltpu.PrefetchScalarGridSpec(
            num_scalar_prefetch=1,
            grid=(f // tn, n_rows // tm),
            in_specs=[pl.BlockSpec((tm, d), lambda j, i, te: (i, 0)), w_spec, w_spec],
            out_specs=pl.BlockSpec((tm, tn), lambda j, i, te: (i, j)),
            scratch_shapes=[pltpu.VMEM((d, tn), BF), pltpu.VMEM((d, tn), BF)],
        ),
        out_shape=jax.ShapeDtypeStruct((n_rows, f), BF),
        compiler_params=_cparams(("arbitrary", "arbitrary"), vmem),
        name="moe_hidden",
    )(tile_expert, us, w1, w3)


def _moe_out_kernel(te_ref, h_ref, w2_ref, o_ref, w2s):
    @pl.when(_expert_changed(te_ref, pl.program_id(1)))
    def _():
        w2s[...] = w2_ref[...].astype(BF)

    o_ref[...] = jnp.dot(h_ref[...], w2s[...], preferred_element_type=F32)


def _moe_out(tile_expert, hs, w2, lead, tm):
    n_rows, f = hs.shape
    d = w2.shape[-1]
    tn = 512
    vmem = 2 * tm * f * 2 + 2 * f * tn * 4 + f * tn * 2 + 3 * tm * tn * 4
    return pl.pallas_call(
        _moe_out_kernel,
        grid_spec=pltpu.PrefetchScalarGridSpec(
            num_scalar_prefetch=1,
            grid=(d // tn, n_rows // tm),
            in_specs=[pl.BlockSpec((tm, f), lambda j, i, te: (i, 0)),
                      pl.BlockSpec((None, None, f, tn), lambda j, i, te: (lead, te[i], 0, j))],
            out_specs=pl.BlockSpec((tm, tn), lambda j, i, te: (i, j)),
            scratch_shapes=[pltpu.VMEM((f, tn), BF)],
        ),
        out_shape=jax.ShapeDtypeStruct((n_rows, d), F32),
        compiler_params=_cparams(("arbitrary", "arbitrary"), vmem),
        name="moe_out",
    )(tile_expert, hs, w2)


def _moe_ln_kernel(p1_ref, p2_ref, x_ref, route_ref, y_hbm, gate_ref, g_ref, b_ref, *rest, emit_u):
    buf, sem = rest[-2:]
    slot = _row_gather_step((p1_ref, p2_ref), y_hbm, buf, sem, x_ref.shape[0])
    route = route_ref[...]
    f = route[:, ROUTE_W1:ROUTE_W1 + 1] * buf[slot, 0] + route[:, ROUTE_W2:ROUTE_W2 + 1] * buf[slot, 1]
    r = DN_ALPHA * x_ref[...] + gate_ref[...] * f
    _ln_tail(r, g_ref, b_ref, rest[:-2], emit_u)


def _moe_combine_ln(pos1, pos2, x2, route, ys, gate, ln_g, ln_b, seq, sc=None, sh=None):
    t, d = x2.shape
    tm = 256
    per_b = seq // tm
    emit_u = sc is not None
    row = pl.BlockSpec((tm, d), lambda i, p1, p2: (i, 0))
    vec = pl.BlockSpec((None, 1, d), lambda i, p1, p2: (i // per_b, 0, 0))
    par = pl.BlockSpec((1, d), lambda i, p1, p2: (0, 0))
    in_specs = [row, pl.BlockSpec((tm, LANES), lambda i, p1, p2: (i, 0)),
                pl.BlockSpec(memory_space=pl.ANY), vec, par, par]
    args = [x2, route, ys, gate, ln_g.reshape(1, d), ln_b.reshape(1, d)]
    out_specs = [row]
    out_shape = [jax.ShapeDtypeStruct((t, d), F32)]
    if emit_u:
        in_specs += [vec, vec]
        args += [sc, sh]
        out_specs.append(row)
        out_shape.append(jax.ShapeDtypeStruct((t, d), BF))
    outs = pl.pallas_call(
        functools.partial(_moe_ln_kernel, emit_u=emit_u),
        grid_spec=pltpu.PrefetchScalarGridSpec(
            num_scalar_prefetch=2,
            grid=(t // tm,),
            in_specs=in_specs,
            out_specs=out_specs,
            scratch_shapes=[pltpu.VMEM((2, 2, tm, d), F32), pltpu.SemaphoreType.DMA((2,))],
        ),
        out_shape=out_shape,
        compiler_params=_cparams(("arbitrary",), 4 * tm * d * 4 + 2 * tm * d * 10 + 4 * tm * d * 4),
        name="moe_combine_ln",
    )(pos1, pos2, *args)
    return outs if emit_u else (outs[0], None)


def kernel(x, c, w_ada, b_ada, w_in, attn_sinks, conv_w, conv_b, dt_bias, a_log, d_skip, ssm_norm_w,
           w_branch, w_o, ln_g, ln_b, ffn_w1, ffn_w3, ffn_w2, router_w, exp_w1, exp_w3, exp_w2):
    nb, seq, d = x.shape
    t = nb * seq
    x2 = x.reshape(t, d)
    mod = _adaln_mod(c, w_ada, b_ada).reshape(DEPTH, nb, 6, 1, d)
    u = _modulate(x2, mod[0, :, 1], mod[0, :, 0], seq)
    for l in range(DEPTH):
        sh1, sc1, g1, sh2, sc2, g2 = (mod[l, :, i] for i in range(6))
        w_tail = _realign_tail(w_in, l)
        q2 = _matmul_ws(u, w_in, (l,), OFF_Q, OFF_KV - OFF_Q, BF, 1024, 512, "proj_q")
        kv2 = _matmul_ws(u, w_in, (l,), OFF_KV, OFF_Z - OFF_KV, BF, 1024, 512, "proj_kv")
        z2 = _matmul_ws(u, w_in, (l,), OFF_Z, OFF_XBC - OFF_Z, BF, 1024, 512, "proj_z")
        xbc2 = _matmul_ws(u, w_in, (l,), OFF_XBC, OFF_DT - OFF_XBC, BF, 1024, 512, "proj_xbc")
        dt2 = _matmul_ws(u, w_in, (l,), OFF_DT, LANES, F32, 1024, LANES, "proj_dt")[:, :SSM_HEADS]
        ret2 = _matmul_ws(u, w_tail, (), 0, RET_WIDTH, BF, 1024, 512, "proj_ret")
        y_attn = _attention(q2, kv2, attn_sinks[l], nb, seq)
        y_ssm = _ssd(z2, xbc2, dt2, conv_w[l], conv_b[l], dt_bias[l], a_log[l], d_skip[l],
                     ssm_norm_w[l], nb, seq)
        y_ret = _retention(ret2, nb, seq)
        merged = _branch_merge(u, y_attn, y_ssm, y_ret, w_tail, w_branch, l)
        mix = _matmul_ws(merged, w_o, (l,), 0, d, F32, 1024, 512, "proj_o")
        sc_next, sh_next = (mod[l + 1, :, 1], mod[l + 1, :, 0]) if l + 1 < DEPTH else (None, None)
        if l % 2 == 0:
            x2, u = _residual_ln(x2, mix, g1, ln_g[l, 0], ln_b[l, 0], seq, sc2, sh2)
            tk_out = 512
            f_pad = pl.cdiv(D_FF_DENSE, tk_out) * tk_out
            hid = _swiglu_hidden(u, ffn_w1, ffn_w3, l // 2, f_pad, 1024, 256, "ffn_hidden")
            f = _matmul_k(hid, ffn_w2, (l // 2,), F32, 2048, 1024, tk_out, "ffn_out")
            x2, u = _residual_ln(x2, f, g2, ln_g[l, 1], ln_b[l, 1], seq, sc_next, sh_next)
        else:
            x2, u32 = _residual_ln(x2, mix, g1, ln_g[l, 0], ln_b[l, 0], seq, sc2, sh2, u_dtype=F32)
            route = _router(u32, router_w[l // 2])
            src_tok, tile_expert, pos1, pos2 = _route_plan(route, MOE_TILE)
            us = _gather_rows(src_tok, u32)
            hs = _moe_hidden(tile_expert, us, exp_w1, exp_w3, l // 2, MOE_TILE)
            ys = _moe_out(tile_expert, hs, exp_w2, l // 2, MOE_TILE)
            x2, u = _moe_combine_ln(pos1, pos2, x2, route, ys, g2, ln_g[l, 1], ln_b[l, 1], seq, sc_next, sh_next)
    return x2.reshape(nb, seq, d)
```

```python
import functools
import math

import jax
import jax.numpy as jnp
from jax import lax
from jax.experimental import pallas as pl
from jax.experimental.pallas import tpu as pltpu

F32 = jnp.float32
BF = jnp.bfloat16
HIGHEST = lax.Precision.HIGHEST

D_MODEL = 4096
DEPTH = 2
ATTN_HEADS = 32
ATTN_KV_HEADS = 4
ATTN_GROUP = ATTN_HEADS // ATTN_KV_HEADS
ATTN_HEAD_DIM = 64
WINDOW = 128
ATTN_SCALE = ATTN_HEAD_DIM ** -0.5
SSM_D_INNER = 2048
SSM_HEAD_DIM = 64
SSM_HEADS = SSM_D_INNER // SSM_HEAD_DIM
SSM_GROUPS = 4
SSM_STATE = 128
SSM_CONV = 4
SSM_CHUNK = 128
SSM_XBC_W = SSM_D_INNER + 2 * SSM_GROUPS * SSM_STATE
SSM_NORM_GROUP = SSM_D_INNER // SSM_GROUPS
RET_HEADS = 8
RET_QK_DIM = 128
RET_V_DIM = 256
RET_CHUNK = 128
N_BRANCH = 3
BRANCH_WIDTH = 2048
D_FF_DENSE = 11008
N_EXPERTS = 8
D_FF_EXPERT = 2048
DN_ALPHA = (2 * DEPTH) ** 0.25
LN_EPS = 1e-5
RMS_EPS = 1e-5

OFF_Q = 0
OFF_KV = OFF_Q + ATTN_HEADS * ATTN_HEAD_DIM
OFF_Z = OFF_KV + 2 * ATTN_KV_HEADS * ATTN_HEAD_DIM
OFF_XBC = OFF_Z + SSM_D_INNER
OFF_DT = OFF_XBC + SSM_XBC_W
OFF_TAIL = OFF_DT + SSM_HEADS
RET_WIDTH = 2 * RET_HEADS * RET_QK_DIM + 2 * RET_HEADS * RET_V_DIM
OFF_GATES_IN_TAIL = RET_WIDTH

LANES = 128
SUBLANES = 8
VMEM_LIMIT_CAP = 56 << 20
MASK_VALUE = -1e30


def _cparams(semantics, vmem_bytes):
    limit = int(min(max(vmem_bytes + (8 << 20), 24 << 20), VMEM_LIMIT_CAP))
    return pltpu.CompilerParams(dimension_semantics=semantics, vmem_limit_bytes=limit)


def _sigmoid(v):
    return 1.0 / (1.0 + jnp.exp(-v))


def _softplus(v):
    return jnp.maximum(v, 0.0) + jnp.log(1.0 + jnp.exp(-jnp.abs(v)))


def _mod_kernel(c_ref, w_ref, b_ref, o_ref):
    c = c_ref[...]
    cond = c * _sigmoid(c)
    o_ref[...] = jnp.dot(cond.astype(BF), w_ref[...].astype(BF),
                         preferred_element_type=F32) + b_ref[...]


def _adaln_mod(c, w_ada, b_ada):
    nb = c.shape[0]
    depth, d, n = w_ada.shape
    tn = 512
    c8 = jnp.pad(c, ((0, SUBLANES - nb), (0, 0)))
    out = pl.pallas_call(
        _mod_kernel,
        grid=(depth, n // tn),
        in_specs=[
            pl.BlockSpec((SUBLANES, d), lambda l, j: (0, 0)),
            pl.BlockSpec((None, d, tn), lambda l, j: (l, 0, j)),
            pl.BlockSpec((None, 1, tn), lambda l, j: (l, 0, j)),
        ],
        out_specs=pl.BlockSpec((None, SUBLANES, tn), lambda l, j: (l, 0, j)),
        out_shape=jax.ShapeDtypeStruct((depth, SUBLANES, n), F32),
        compiler_params=_cparams(("arbitrary", "arbitrary"), 2 * d * tn * 4 + d * tn * 2),
        name="adaln_mod",
    )(c8, w_ada, b_ada.reshape(depth, 1, n))
    return out[:, :nb, :]


def _modulate_kernel(x_ref, sc_ref, sh_ref, o_ref):
    o_ref[...] = (x_ref[...] * (1.0 + sc_ref[...]) + sh_ref[...]).astype(BF)


def _modulate(x2, sc, sh, seq):
    t, d = x2.shape
    tm = 512
    per_b = seq // tm
    vec = pl.BlockSpec((None, 1, d), lambda i: (i // per_b, 0, 0))
    return pl.pallas_call(
        _modulate_kernel,
        grid=(t // tm,),
        in_specs=[pl.BlockSpec((tm, d), lambda i: (i, 0)), vec, vec],
        out_specs=pl.BlockSpec((tm, d), lambda i: (i, 0)),
        out_shape=jax.ShapeDtypeStruct((t, d), BF),
        compiler_params=_cparams(("arbitrary",), 2 * tm * d * 6),
        name="modulate",
    )(x2, sc, sh)


def _wt_rows_spec(layer, row0, tn, k, **kwargs):
    assert row0 % SUBLANES == 0 and tn % SUBLANES == 0

    def index_map(*idx):
        return (layer, pl.multiple_of(row0 + idx[0] * tn, SUBLANES), 0)

    return pl.BlockSpec((pl.Element(1), pl.Element(tn), pl.Element(k)), index_map, **kwargs)


def _ln_tail(r, g_ref, b_ref, rest, emit_u):
    mu = jnp.mean(r, axis=-1, keepdims=True)
    dlt = r - mu
    var = jnp.mean(dlt * dlt, axis=-1, keepdims=True)
    xn = dlt * lax.rsqrt(var + LN_EPS) * g_ref[...] + b_ref[...]
    if emit_u:
        sc_ref, sh_ref, xo_ref, uo_ref = rest
        xo_ref[...] = xn
        uo_ref[...] = (xn * (1.0 + sc_ref[...]) + sh_ref[...]).astype(uo_ref.dtype)
    else:
        (xo_ref,) = rest
        xo_ref[...] = xn


def _ln_kernel(x_ref, y_ref, gate_ref, g_ref, b_ref, *rest, emit_u):
    r = DN_ALPHA * x_ref[...] + gate_ref[...] * y_ref[...]
    _ln_tail(r, g_ref, b_ref, rest, emit_u)


def _residual_ln(x2, y2, gate, ln_g, ln_b, seq, sc=None, sh=None):
    t, d = x2.shape
    tm = 256
    per_b = seq // tm
    emit_u = sc is not None
    row = pl.BlockSpec((tm, d), lambda i: (i, 0))
    vec = pl.BlockSpec((None, 1, d), lambda i: (i // per_b, 0, 0))
    par = pl.BlockSpec((1, d), lambda i: (0, 0))
    in_specs = [row, row, vec, par, par]
    args = [x2, y2, gate, ln_g.reshape(1, d), ln_b.reshape(1, d)]
    out_specs = [row]
    out_shape = [jax.ShapeDtypeStruct((t, d), F32)]
    if emit_u:
        in_specs += [vec, vec]
        args += [sc, sh]
        out_specs.append(row)
        out_shape.append(jax.ShapeDtypeStruct((t, d), BF))
    outs = pl.pallas_call(
        functools.partial(_ln_kernel, emit_u=emit_u),
        grid=(t // tm,),
        in_specs=in_specs,
        out_specs=out_specs,
        out_shape=out_shape,
        compiler_params=_cparams(("arbitrary",), 2 * tm * d * 14 + 4 * tm * d * 4),
        name="residual_ln",
    )(*args)
    return outs if emit_u else (outs[0], None)


def _mm_ws_kernel(a_ref, b_ref, o_ref, *scratch):
    if scratch:
        (wbf,) = scratch

        @pl.when(pl.program_id(1) == 0)
        def _():
            wbf[...] = b_ref[...].astype(BF)

        w = wbf[...]
    else:
        w = b_ref[...]
    o_ref[...] = jnp.dot(a_ref[...], w, preferred_element_type=F32).astype(o_ref.dtype)


def _matmul_ws(a, w, lead, col0, n, out_dtype, tm, tn, name):
    m, k = a.shape
    joff = col0 // tn
    assert col0 % tn == 0 and n % tn == 0 and m % tm == 0
    nlead = len(lead)
    w_spec = pl.BlockSpec((None,) * nlead + (k, tn), lambda j, i: tuple(lead) + (0, j + joff))
    cast = w.dtype != BF
    scratch = [pltpu.VMEM((k, tn), BF)] if cast else []
    osz = jnp.dtype(out_dtype).itemsize
    vmem = 2 * tm * k * 2 + 2 * k * tn * w.dtype.itemsize + k * tn * 2 * cast + 2 * tm * tn * osz + tm * tn * 4
    return pl.pallas_call(
        _mm_ws_kernel,
        grid=(n // tn, m // tm),
        in_specs=[pl.BlockSpec((tm, k), lambda j, i: (i, 0)), w_spec],
        out_specs=pl.BlockSpec((tm, tn), lambda j, i: (i, j)),
        out_shape=jax.ShapeDtypeStruct((m, n), out_dtype),
        scratch_shapes=scratch,
        compiler_params=_cparams(("arbitrary", "arbitrary"), vmem),
        name=name,
    )(a, w)


def _mm_wt_kernel(a_ref, w_ref, o_ref, wbf):
    @pl.when(pl.program_id(1) == 0)
    def _():
        wbf[...] = w_ref[0].T.astype(BF)

    o_ref[...] = jnp.dot(a_ref[...], wbf[...], preferred_element_type=F32).astype(o_ref.dtype)


def _matmul_wt(a, wt, layer, row0, n, out_dtype, tm, tn, name):
    m, k = a.shape
    assert n % tn == 0 and m % tm == 0
    osz = jnp.dtype(out_dtype).itemsize
    vmem = 2 * tm * k * 2 + 3 * k * tn * 4 + k * tn * 2 + 2 * tm * tn * osz + tm * tn * 4
    return pl.pallas_call(
        _mm_wt_kernel,
        grid=(n // tn, m // tm),
        in_specs=[pl.BlockSpec((tm, k), lambda j, i: (i, 0)), _wt_rows_spec(layer, row0, tn, k)],
        out_specs=pl.BlockSpec((tm, tn), lambda j, i: (i, j)),
        out_shape=jax.ShapeDtypeStruct((m, n), out_dtype),
        scratch_shapes=[pltpu.VMEM((k, tn), BF)],
        compiler_params=_cparams(("arbitrary", "arbitrary"), vmem),
        name=name,
    )(a, wt)


def _mm_k_kernel(a_ref, b_ref, o_ref, acc_ref, *, k_valid):
    kk = pl.program_id(2)
    tk = b_ref.shape[0]

    @pl.when(kk == 0)
    def _():
        acc_ref[...] = jnp.zeros_like(acc_ref)

    b = b_ref[...]
    if k_valid % tk:
        row = lax.broadcasted_iota(jnp.int32, b.shape, 0) + kk * tk
        b = jnp.where(row < k_valid, b, 0.0)
    acc_ref[...] += jnp.dot(a_ref[...], b.astype(BF), preferred_element_type=F32)

    @pl.when(kk == pl.num_programs(2) - 1)
    def _():
        o_ref[...] = acc_ref[...].astype(o_ref.dtype)


def _matmul_k(a, w, lead, out_dtype, tm, tn, tk, name):
    m, k = a.shape
    k_valid, n = w.shape[-2:]
    assert m % tm == 0 and n % tn == 0 and k % tk == 0 and k - tk < k_valid <= k
    nlead = len(lead)
    w_spec = pl.BlockSpec((None,) * nlead + (tk, tn), lambda i, j, kk: tuple(lead) + (kk, j))
    osz = jnp.dtype(out_dtype).itemsize
    vmem = 2 * tm * tk * 2 + 2 * tk * tn * 4 + tk * tn * 2 + 2 * tm * tn * osz + 2 * tm * tn * 4
    return pl.pallas_call(
        functools.partial(_mm_k_kernel, k_valid=k_valid),
        grid=(m // tm, n // tn, k // tk),
        in_specs=[pl.BlockSpec((tm, tk), lambda i, j, kk: (i, kk)), w_spec],
        out_specs=pl.BlockSpec((tm, tn), lambda i, j, kk: (i, j)),
        out_shape=jax.ShapeDtypeStruct((m, n), out_dtype),
        scratch_shapes=[pltpu.VMEM((tm, tn), F32)],
        compiler_params=_cparams(("arbitrary", "arbitrary", "arbitrary"), vmem),
        name=name,
    )(a, w)


def _attn_kernel(q_ref, kvp_ref, kvc_ref, bp_ref, bc_ref, sink_ref, o_ref):
    nblk = pl.program_id(1)
    blk = q_ref.shape[0]
    dh = ATTN_HEAD_DIM
    kvw = ATTN_KV_HEADS * dh
    q = q_ref[...]
    kvp = kvp_ref[...]
    kvc = kvc_ref[...]
    dims = (((1,), (1,)), ((), ()))
    ones = jnp.ones((blk, blk), BF)
    outs = []
    for kh in range(ATTN_KV_HEADS):
        heads = range(kh * ATTN_GROUP, (kh + 1) * ATTN_GROUP)
        qs = jnp.concatenate([q[:, h * dh:(h + 1) * dh] for h in heads], axis=0)
        kcol = slice(kh * dh, (kh + 1) * dh)
        vcol = slice(kvw + kh * dh, kvw + (kh + 1) * dh)
        sp = lax.dot_general(qs, kvp[:, kcol], dims, preferred_element_type=F32).reshape(ATTN_GROUP, blk, blk)
        sc = lax.dot_general(qs, kvc[:, kcol], dims, preferred_element_type=F32).reshape(ATTN_GROUP, blk, blk)
        sp = sp * ATTN_SCALE + bp_ref[kh]
        sp = jnp.where(nblk == 0, MASK_VALUE, sp)
        sc = sc * ATTN_SCALE + bc_ref[kh]
        sink = sink_ref[kh][:, :, :1]
        m = jnp.maximum(jnp.max(jnp.maximum(sp, sc), axis=-1, keepdims=True), sink)
        pp = jnp.exp(sp - m).astype(BF).reshape(ATTN_GROUP * blk, blk)
        pc = jnp.exp(sc - m).astype(BF).reshape(ATTN_GROUP * blk, blk)
        psum = (jnp.dot(pp, ones, preferred_element_type=F32)
                + jnp.dot(pc, ones, preferred_element_type=F32)).reshape(ATTN_GROUP, blk, blk)
        den = psum[:, :, :dh] + jnp.exp(sink - m)
        o = jnp.dot(pp, kvp[:, vcol], preferred_element_type=F32)
        o = o + jnp.dot(pc, kvc[:, vcol], preferred_element_type=F32)
        o = o.reshape(ATTN_GROUP, blk, dh) / den
        outs.extend(o[g] for g in range(ATTN_GROUP))
    o_ref[...] = jnp.concatenate(outs, axis=1).astype(o_ref.dtype)


def _attention(q2, kv2, sinks, nb, seq):
    blk = WINDOW
    nblk = seq // blk
    qw = ATTN_HEADS * ATTN_HEAD_DIM
    kvw2 = 2 * ATTN_KV_HEADS * ATTN_HEAD_DIM
    slopes = jnp.exp2(-8.0 * jnp.arange(1, ATTN_HEADS + 1, dtype=F32) / ATTN_HEADS)
    slopes = slopes.reshape(ATTN_KV_HEADS, ATTN_GROUP, 1, 1)
    qi = jnp.arange(blk)[:, None]
    kj = jnp.arange(blk)[None, :]
    dist_p = blk + qi - kj
    dist_c = qi - kj
    bias_p = jnp.where(dist_p < WINDOW, -slopes * dist_p.astype(F32), MASK_VALUE)
    bias_c = jnp.where(dist_c >= 0, -slopes * dist_c.astype(F32), MASK_VALUE)
    sink_t = jnp.broadcast_to(sinks.astype(F32).reshape(ATTN_KV_HEADS, ATTN_GROUP, 1, 1),
                              (ATTN_KV_HEADS, ATTN_GROUP, 1, LANES))
    row = lambda w: pl.BlockSpec((blk, w), lambda b, n: (b * nblk + n, 0))
    prev = pl.BlockSpec((blk, kvw2), lambda b, n: (b * nblk + jnp.maximum(n - 1, 0), 0))
    tab = pl.BlockSpec((ATTN_KV_HEADS, ATTN_GROUP, blk, blk), lambda b, n: (0, 0, 0, 0))
    sk = pl.BlockSpec((ATTN_KV_HEADS, ATTN_GROUP, 1, LANES), lambda b, n: (0, 0, 0, 0))
    return pl.pallas_call(
        _attn_kernel,
        grid=(nb, nblk),
        in_specs=[row(qw), prev, row(kvw2), tab, tab, sk],
        out_specs=row(qw),
        out_shape=jax.ShapeDtypeStruct((nb * seq, qw), BF),
        compiler_params=_cparams(("arbitrary", "arbitrary"), 24 << 20),
        name="swa_attention",
    )(q2, kv2, kv2, bias_p, bias_c, sink_t)


def _ssd_kernel(z_ref, x_ref, dt_ref, dtt_ref, cw_ref, cb_ref, dtb_ref, dtbt_ref, al_ref, alt_ref,
                dsk_ref, nw_ref, e_ref, o_ref, xbuf, state, yscr):
    q_len = SSM_CHUNK
    halo = SUBLANES

    @pl.when(pl.program_id(1) == 0)
    def _():
        xbuf[0:halo, :] = jnp.zeros((halo, SSM_XBC_W), F32)
        state[...] = jnp.zeros_like(state)

    xbuf[halo:halo + q_len, :] = x_ref[...].astype(F32)
    acc = cb_ref[...] + cw_ref[0:1, :] * xbuf[halo - 3:halo - 3 + q_len, :]
    for j in range(1, SSM_CONV):
        acc = acc + cw_ref[j:j + 1, :] * xbuf[halo - 3 + j:halo - 3 + j + q_len, :]
    xbuf[0:halo, :] = xbuf[q_len:q_len + halo, :]
    xbc = acc * _sigmoid(acc)
    xs = xbc[:, :SSM_D_INNER]
    bm = xbc[:, SSM_D_INNER:SSM_D_INNER + SSM_GROUPS * SSM_STATE]
    cm = xbc[:, SSM_D_INNER + SSM_GROUPS * SSM_STATE:]

    dt = _softplus(dt_ref[...] + dtb_ref[...])
    la = dt * (-jnp.exp(al_ref[...]))
    la_t = _softplus(dtt_ref[...] + dtbt_ref[...]) * (-jnp.exp(alt_ref[...]))
    row = lax.broadcasted_iota(jnp.int32, (q_len, q_len), 0)
    col = lax.broadcasted_iota(jnp.int32, (q_len, q_len), 1)
    causal = col <= row
    cs = jnp.dot(causal.astype(F32), la, precision=HIGHEST, preferred_element_type=F32)
    cs_t = jnp.dot(la_t, (row <= col).astype(F32), precision=HIGHEST, preferred_element_type=F32)
    ecs = jnp.exp(cs)
    dte = jnp.exp(cs[q_len - 1:q_len, :] - cs)
    stack = jnp.concatenate([dt, dt * dte, ecs], axis=0)
    hi = stack.astype(BF)
    lo = (stack - hi.astype(F32)).astype(BF)
    wide = (jnp.dot(hi, e_ref[...], preferred_element_type=F32)
            + jnp.dot(lo, e_ref[...], preferred_element_type=F32))
    w_dt = wide[0:q_len]
    w_dd = wide[q_len:2 * q_len]
    w_e = wide[2 * q_len:3 * q_len]
    x_dt = (xs * w_dt).astype(BF)
    x_dd = (xs * w_dd).astype(BF)
    left = lax.broadcasted_iota(jnp.int32, (q_len, LANES), 1) < SSM_HEAD_DIM
    heads_per_group = SSM_HEADS // SSM_GROUPS
    for g in range(SSM_GROUPS):
        bg = bm[:, g * SSM_STATE:(g + 1) * SSM_STATE]
        cg = cm[:, g * SSM_STATE:(g + 1) * SSM_STATE].astype(BF)
        cbm = lax.dot_general(cg, bg.astype(BF), (((1,), (1,)), ((), ())), preferred_element_type=F32)
        bg_t = bg.T.astype(BF)
        for jp in range(heads_per_group // 2):
            p = g * (heads_per_group // 2) + jp
            h0 = 2 * p
            sl = slice(p * LANES, (p + 1) * LANES)
            l0 = jnp.exp(jnp.where(causal, cs[:, h0:h0 + 1] - cs_t[h0:h0 + 1, :], -jnp.inf))
            l1 = jnp.exp(jnp.where(causal, cs[:, h0 + 1:h0 + 2] - cs_t[h0 + 1:h0 + 2, :], -jnp.inf))
            xp = x_dt[:, sl]
            y0 = jnp.dot((cbm * l0).astype(BF), xp, preferred_element_type=F32)
            y1 = jnp.dot((cbm * l1).astype(BF), xp, preferred_element_type=F32)
            st = state[p]
            y_off = jnp.dot(cg, st.astype(BF), preferred_element_type=F32)
            yscr[:, sl] = jnp.where(left, y0, y1) + y_off * w_e[:, sl] + dsk_ref[:, sl] * xs[:, sl]
            state[p] = st * w_e[q_len - 1:q_len, sl] + jnp.dot(bg_t, x_dd[:, sl], preferred_element_type=F32)

    zf = z_ref[...].astype(F32)
    yg = yscr[...] * (zf * _sigmoid(zf))
    for g in range(SSM_GROUPS):
        gs = slice(g * SSM_NORM_GROUP, (g + 1) * SSM_NORM_GROUP)
        blk = yg[:, gs]
        ms = jnp.mean(blk * blk, axis=-1, keepdims=True)
        o_ref[:, gs] = (blk * lax.rsqrt(ms + RMS_EPS) * nw_ref[:, gs]).astype(o_ref.dtype)


def _ssd(z2, xbc2, dt2, conv_w, conv_b, dt_bias, a_log, d_skip, norm_w, nb, seq):
    t = nb * seq
    q_len = SSM_CHUNK
    nchunk = seq // q_len
    dt_t = dt2.reshape(nb, seq, SSM_HEADS).transpose(0, 2, 1)
    expand = jnp.repeat(jnp.eye(SSM_HEADS, dtype=BF), SSM_HEAD_DIM, axis=1)
    dsk = jnp.repeat(d_skip.astype(F32), SSM_HEAD_DIM).reshape(1, SSM_D_INNER)
    rowblk = lambda w: pl.BlockSpec((q_len, w), lambda b, c: (b * nchunk + c, 0))
    full = lambda r, w: pl.BlockSpec((r, w), lambda b, c: (0, 0))
    return pl.pallas_call(
        _ssd_kernel,
        grid=(nb, nchunk),
        in_specs=[
            rowblk(SSM_D_INNER), rowblk(SSM_XBC_W), rowblk(SSM_HEADS),
            pl.BlockSpec((None, SSM_HEADS, q_len), lambda b, c: (b, 0, c)),
            full(SSM_CONV, SSM_XBC_W), full(1, SSM_XBC_W),
            full(1, SSM_HEADS), full(SSM_HEADS, 1), full(1, SSM_HEADS), full(SSM_HEADS, 1),
            full(1, SSM_D_INNER), full(1, SSM_D_INNER), full(SSM_HEADS, SSM_D_INNER),
        ],
        out_specs=rowblk(SSM_D_INNER),
        out_shape=jax.ShapeDtypeStruct((t, SSM_D_INNER), BF),
        scratch_shapes=[
            pltpu.VMEM((q_len + 2 * SUBLANES, SSM_XBC_W), F32),
            pltpu.VMEM((SSM_HEADS // 2, SSM_STATE, LANES), F32),
            pltpu.VMEM((q_len, SSM_D_INNER), F32),
        ],
        compiler_params=_cparams(("arbitrary", "arbitrary"), 32 << 20),
        name="mamba2_ssd",
    )(z2, xbc2, dt2, dt_t, conv_w, conv_b.reshape(1, SSM_XBC_W),
      dt_bias.reshape(1, SSM_HEADS), dt_bias.reshape(SSM_HEADS, 1),
      a_log.reshape(1, SSM_HEADS), a_log.reshape(SSM_HEADS, 1),
      dsk, norm_w.reshape(1, SSM_D_INNER), expand)


def _ret_kernel(q_ref, k_ref, v_ref, g_ref, dm_ref, kd_ref, qd_ref, cd_ref, o_ref, rstate):
    @pl.when(pl.program_id(1) == 0)
    def _():
        rstate[...] = jnp.zeros_like(rstate)

    for h in range(RET_HEADS):
        ks = slice(h * RET_QK_DIM, (h + 1) * RET_QK_DIM)
        vs = slice(h * RET_V_DIM, (h + 1) * RET_V_DIM)
        qh = q_ref[:, ks]
        kf = k_ref[:, ks].astype(F32) * (RET_QK_DIM ** -0.5)
        vh = v_ref[:, vs]
        sc = lax.dot_general(qh, kf.astype(BF), (((1,), (1,)), ((), ())), preferred_element_type=F32)
        sc = sc * dm_ref[h]
        inner = jnp.dot(sc.astype(BF), vh, preferred_element_type=F32)
        r = rstate[h]
        cross = jnp.dot(qh, r.astype(BF), preferred_element_type=F32) * qd_ref[:, vs]
        y = inner + cross
        k2t = (kf * kd_ref[:, ks]).T.astype(BF)
        rstate[h] = r * cd_ref[:, vs] + jnp.dot(k2t, vh, preferred_element_type=F32)
        mu = jnp.mean(y, axis=-1, keepdims=True)
        dlt = y - mu
        var = jnp.mean(dlt * dlt, axis=-1, keepdims=True)
        gf = g_ref[:, vs].astype(F32)
        o_ref[:, vs] = (gf * _sigmoid(gf) * (dlt * lax.rsqrt(var + LN_EPS))).astype(o_ref.dtype)


def _retention(ret2, nb, seq):
    t = nb * seq
    c_len = RET_CHUNK
    nchunk = seq // c_len
    qk_w = RET_HEADS * RET_QK_DIM
    v_w = RET_HEADS * RET_V_DIM
    log_gamma = jnp.log1p(-jnp.exp2(-5.0 - jnp.arange(RET_HEADS, dtype=F32)))
    pos = jnp.arange(c_len, dtype=F32)
    diff = pos[:, None] - pos[None, :]
    dmask = jnp.where(diff >= 0, jnp.exp(log_gamma[:, None, None] * jnp.maximum(diff, 0.0)), 0.0)
    k_decay = jnp.exp((c_len - 1.0 - pos)[:, None] * log_gamma[None, :])
    q_decay = jnp.exp((pos + 1.0)[:, None] * log_gamma[None, :])
    chunk_decay = jnp.exp(log_gamma * c_len)
    kd = jnp.repeat(k_decay, RET_QK_DIM, axis=1)
    qd = jnp.repeat(q_decay, RET_V_DIM, axis=1)
    cd = jnp.repeat(chunk_decay, RET_V_DIM).reshape(1, v_w)
    blk = lambda w, off: pl.BlockSpec((c_len, w), lambda b, c: (b * nchunk + c, off))
    return pl.pallas_call(
        _ret_kernel,
        grid=(nb, nchunk),
        in_specs=[
            blk(qk_w, 0), blk(qk_w, 1), blk(v_w, 1), blk(v_w, 2),
            pl.BlockSpec((RET_HEADS, c_len, c_len), lambda b, c: (0, 0, 0)),
            pl.BlockSpec((c_len, qk_w), lambda b, c: (0, 0)),
            pl.BlockSpec((c_len, v_w), lambda b, c: (0, 0)),
            pl.BlockSpec((1, v_w), lambda b, c: (0, 0)),
        ],
        out_specs=pl.BlockSpec((c_len, v_w), lambda b, c: (b * nchunk + c, 0)),
        out_shape=jax.ShapeDtypeStruct((t, v_w), BF),
        scratch_shapes=[pltpu.VMEM((RET_HEADS, RET_QK_DIM, RET_V_DIM), F32)],
        compiler_params=_cparams(("arbitrary", "arbitrary"), 24 << 20),
        name="retention",
    )(ret2, ret2, ret2, ret2, dmask, kd, qd, cd)


def _merge_kernel(u_ref, ya_ref, ys_ref, yr_ref, wg0_ref, wg1_ref, wg2_ref, wb_ref, o_ref, wgs, wbs):
    @pl.when(pl.program_id(1) == 0)
    def _():
        for r, wg_ref in enumerate((wg0_ref, wg1_ref, wg2_ref)):
            wgs[r] = wg_ref[0].T.astype(BF)
        wbs[...] = wb_ref[...].astype(BF)

    u = u_ref[...]
    acc = None
    for r, y_ref in enumerate((ya_ref, ys_ref, yr_ref)):
        gate = jnp.dot(u, wgs[r], preferred_element_type=F32)
        wide = jnp.dot(y_ref[...], wbs[r], preferred_element_type=F32)
        term = _sigmoid(gate) * wide
        acc = term if acc is None else acc + term
    o_ref[...] = acc.astype(o_ref.dtype)


def _branch_merge(u, y_attn, y_ssm, y_ret, wt, w_branch, layer):
    t, d = u.shape
    tm, tn = 512, 256
    gate_row0 = OFF_TAIL + OFF_GATES_IN_TAIL
    ub = pl.BlockSpec((tm, d), lambda j, i: (i, 0))
    yb = pl.BlockSpec((tm, BRANCH_WIDTH), lambda j, i: (i, 0))
    once = dict(pipeline_mode=pl.Buffered(1))
    wg = lambda r: _wt_rows_spec(layer, gate_row0 + r * D_MODEL, tn, d, **once)
    wb = pl.BlockSpec((None, N_BRANCH, BRANCH_WIDTH, tn), lambda j, i: (layer, 0, 0, j), **once)
    vmem = 2 * (tm * d * 2 + 3 * tm * BRANCH_WIDTH * 2) + 4 * d * tn * 4 + 3 * BRANCH_WIDTH * tn * 4
    vmem += 3 * d * tn * 2 + 3 * BRANCH_WIDTH * tn * 2 + 8 * tm * tn * 4
    return pl.pallas_call(
        _merge_kernel,
        grid=(D_MODEL // tn, t // tm),
        in_specs=[ub, yb, yb, yb, wg(0), wg(1), wg(2), wb],
        out_specs=pl.BlockSpec((tm, tn), lambda j, i: (i, j)),
        out_shape=jax.ShapeDtypeStruct((t, D_MODEL), BF),
        scratch_shapes=[pltpu.VMEM((N_BRANCH, d, tn), BF), pltpu.VMEM((N_BRANCH, BRANCH_WIDTH, tn), BF)],
        compiler_params=_cparams(("arbitrary", "arbitrary"), vmem),
        name="branch_merge",
    )(u, y_attn, y_ssm, y_ret, wt, wt, wt, w_branch)


def _swiglu_kernel(u_ref, w1_ref, w3_ref, o_ref, w1s, w3s, *, n_blocks):
    j = pl.program_id(0)

    @pl.when((pl.program_id(1) == 0) & (j < n_blocks))
    def _():
        w1s[...] = w1_ref[...].astype(BF)
        w3s[...] = w3_ref[...].astype(BF)

    @pl.when(j < n_blocks)
    def _():
        u = u_ref[...]
        a = jnp.dot(u, w1s[...], preferred_element_type=F32)
        b = jnp.dot(u, w3s[...], preferred_element_type=F32)
        o_ref[...] = (a * _sigmoid(a) * b).astype(o_ref.dtype)

    @pl.when(j >= n_blocks)
    def _():
        o_ref[...] = jnp.zeros_like(o_ref)


def _swiglu_hidden(u, w1, w3, lead, f_out, tm, tn, name):
    t, d = u.shape
    f = w1.shape[-1]
    assert f % tn == 0 and f_out % tn == 0 and f_out >= f
    n_blocks = f // tn
    w_spec = pl.BlockSpec((None, d, tn), lambda j, i: (lead, 0, jnp.minimum(j, n_blocks - 1)))
    vmem = 2 * tm * d * 2 + 4 * d * tn * 4 + 2 * d * tn * 2 + 2 * tm * tn * 2 + 4 * tm * tn * 4
    return pl.pallas_call(
        functools.partial(_swiglu_kernel, n_blocks=n_blocks),
        grid=(f_out // tn, t // tm),
        in_specs=[pl.BlockSpec((tm, d), lambda j, i: (i, 0)), w_spec, w_spec],
        out_specs=pl.BlockSpec((tm, tn), lambda j, i: (i, j)),
        out_shape=jax.ShapeDtypeStruct((t, f_out), BF),
        scratch_shapes=[pltpu.VMEM((d, tn), BF), pltpu.VMEM((d, tn), BF)],
        compiler_params=_cparams(("arbitrary", "arbitrary"), vmem),
        name=name,
    )(u, w1, w3)


MOE_TILE = 512
ROUTE_W1, ROUTE_W2, ROUTE_E1, ROUTE_E2 = 0, 1, 2, 3


def _route_record(logits):
    lane = lax.broadcasted_iota(jnp.int32, logits.shape, 1)
    lg = jnp.where(lane < N_EXPERTS, logits, -jnp.inf)
    m1 = jnp.max(lg, axis=-1, keepdims=True)
    i1 = jnp.min(jnp.where(lg == m1, lane, LANES), axis=-1, keepdims=True)
    lg2 = jnp.where(lane == i1, -jnp.inf, lg)
    m2 = jnp.max(lg2, axis=-1, keepdims=True)
    i2 = jnp.min(jnp.where(lg2 == m2, lane, LANES), axis=-1, keepdims=True)
    e2 = jnp.exp(m2 - m1)
    inv = 1.0 / (1.0 + e2)
    rec = jnp.where(lane == ROUTE_W1, inv, 0.0)
    rec = jnp.where(lane == ROUTE_W2, e2 * inv, rec)
    rec = jnp.where(lane == ROUTE_E1, i1.astype(F32), rec)
    rec = jnp.where(lane == ROUTE_E2, i2.astype(F32), rec)
    return rec


def _ln_route_kernel(x_ref, y_ref, gate_ref, g_ref, b_ref, sc_ref, sh_ref, rwh_ref, rwl_ref,
                     xo_ref, u3_ref, route_ref):
    r = DN_ALPHA * x_ref[...] + gate_ref[...] * y_ref[...]
    mu = jnp.mean(r, axis=-1, keepdims=True)
    dlt = r - mu
    var = jnp.mean(dlt * dlt, axis=-1, keepdims=True)
    xn = dlt * lax.rsqrt(var + LN_EPS) * g_ref[...] + b_ref[...]
    xo_ref[...] = xn
    u = xn * (1.0 + sc_ref[...]) + sh_ref[...]
    tm, d = u.shape
    slabs = d // LANES
    for s in range(slabs):
        u3_ref[pl.ds(s, tm, stride=slabs), :] = u[:, s * LANES:(s + 1) * LANES]
    u_hi = u.astype(BF)
    u_lo = (u - u_hi.astype(F32)).astype(BF)
    logits = (jnp.dot(u_hi, rwh_ref[...], preferred_element_type=F32)
              + jnp.dot(u_lo, rwh_ref[...], preferred_element_type=F32)
              + jnp.dot(u_hi, rwl_ref[...], preferred_element_type=F32))
    route_ref[...] = _route_record(logits)


def _residual_ln_route(x2, y2, gate, ln_g, ln_b, seq, sc, sh, router_w):
    t, d = x2.shape
    tm = 256
    per_b = seq // tm
    slabs = d // LANES
    rw = jnp.pad(router_w, ((0, 0), (0, LANES - N_EXPERTS)))
    rw_hi = rw.astype(BF)
    rw_lo = (rw - rw_hi.astype(F32)).astype(BF)
    row = pl.BlockSpec((tm, d), lambda i: (i, 0))
    vec = pl.BlockSpec((None, 1, d), lambda i: (i // per_b, 0, 0))
    par = pl.BlockSpec((1, d), lambda i: (0, 0))
    rws = pl.BlockSpec((d, LANES), lambda i: (0, 0))
    return pl.pallas_call(
        _ln_route_kernel,
        grid=(t // tm,),
        in_specs=[row, row, vec, par, par, vec, vec, rws, rws],
        out_specs=[row, pl.BlockSpec((tm * slabs, LANES), lambda i: (i, 0)),
                   pl.BlockSpec((tm, LANES), lambda i: (i, 0))],
        out_shape=[jax.ShapeDtypeStruct((t, d), F32), jax.ShapeDtypeStruct((t * slabs, LANES), F32),
                   jax.ShapeDtypeStruct((t, LANES), F32)],
        compiler_params=_cparams(("arbitrary",), 2 * tm * d * 16 + 6 * tm * d * 4 + 2 * d * LANES * 4),
        name="residual_ln_route",
    )(x2, y2, gate, ln_g.reshape(1, d), ln_b.reshape(1, d), sc, sh, rw_hi, rw_lo)


def _route_plan(route, tm):
    t = route.shape[0]
    n_rows = 2 * t + N_EXPERTS * tm
    e_flat = jnp.concatenate([route[:, ROUTE_E1], route[:, ROUTE_E2]]).astype(jnp.int32)
    onehot = (e_flat[:, None] == jnp.arange(N_EXPERTS, dtype=jnp.int32)[None, :]).astype(jnp.int32)
    csum = jnp.cumsum(onehot, axis=0)
    rank = jnp.sum(csum * onehot, axis=1) - 1
    counts = csum[-1]
    padded = ((counts + tm - 1) // tm) * tm
    p_end = jnp.cumsum(padded)
    pos = (p_end - padded)[e_flat] + rank
    src_tok = jnp.zeros((n_rows,), jnp.int32).at[pos].set(jnp.arange(2 * t, dtype=jnp.int32) % t)
    n_tiles = n_rows // tm
    tile_start = jnp.arange(n_tiles, dtype=jnp.int32) * tm
    tile_expert = jnp.minimum(jnp.searchsorted(p_end, tile_start, side="right"), N_EXPERTS - 1).astype(jnp.int32)
    meta = jnp.concatenate([tile_expert, (p_end[-1:] // tm).astype(jnp.int32)])
    return src_tok, meta, pos[:t], pos[t:]


def _row_gather_issue(idx_refs, src_hbm, buf, sem, tile, slot, rows, span, pitch):
    def body(r, carry):
        for which, idx_ref in enumerate(idx_refs):
            p = idx_ref[tile * rows + r]
            pltpu.make_async_copy(src_hbm.at[pl.ds(pl.multiple_of(p * span, span), span), :],
                                  buf.at[slot, which, pl.ds(pl.multiple_of(r * pitch, math.gcd(span, pitch)), span), :],
                                  sem.at[slot]).start()
        return carry

    lax.fori_loop(0, rows, body, 0, unroll=8)


def _row_gather_step(idx_refs, src_hbm, buf, sem, rows, span=1, pitch=1, n_tiles=None):
    i = pl.program_id(0)
    if n_tiles is None:
        n_tiles = pl.num_programs(0)

    @pl.when((i == 0) & (n_tiles > 0))
    def _():
        _row_gather_issue(idx_refs, src_hbm, buf, sem, 0, 0, rows, span, pitch)

    @pl.when(i + 1 < n_tiles)
    def _():
        _row_gather_issue(idx_refs, src_hbm, buf, sem, i + 1, (i + 1) % 2, rows, span, pitch)

    slot = i % 2

    @pl.when(i < n_tiles)
    def _():
        for which in range(len(idx_refs)):
            pltpu.make_async_copy(src_hbm.at[pl.ds(0, rows * span), :],
                                  buf.at[slot, which, pl.ds(0, rows * span), :], sem.at[slot]).wait()

    return slot


GATHER_PITCH_PAD = SUBLANES


def _gather_kernel(src_ref, lim_ref, u3_hbm, o_ref, buf, sem):
    tg, d = o_ref.shape
    slabs = d // LANES
    pitch = slabs + GATHER_PITCH_PAD
    n_tiles = lim_ref[0]
    slot = _row_gather_step((src_ref,), u3_hbm, buf, sem, tg, slabs, pitch, n_tiles)

    @pl.when(pl.program_id(0) < n_tiles)
    def _():
        for s in range(slabs):
            o_ref[:, s * LANES:(s + 1) * LANES] = buf[slot, 0, pl.ds(s, tg, stride=pitch), :].astype(o_ref.dtype)

    @pl.when(pl.program_id(0) >= n_tiles)
    def _():
        o_ref[...] = jnp.zeros_like(o_ref)


def _gather_rows(src_tok, n_active_rows, u3, d):
    n_rows = src_tok.shape[0]
    slabs = d // LANES
    tg = 256
    limit = ((n_active_rows + tg - 1) // tg).astype(jnp.int32).reshape(1)
    return pl.pallas_call(
        _gather_kernel,
        grid_spec=pltpu.PrefetchScalarGridSpec(
            num_scalar_prefetch=2,
            grid=(n_rows // tg,),
            in_specs=[pl.BlockSpec(memory_space=pl.ANY)],
            out_specs=pl.BlockSpec((tg, d), lambda i, src, lim: (i, 0)),
            scratch_shapes=[pltpu.VMEM((2, 1, tg * (slabs + GATHER_PITCH_PAD), LANES), F32),
                            pltpu.SemaphoreType.DMA((2,))],
        ),
        out_shape=jax.ShapeDtypeStruct((n_rows, d), BF),
        compiler_params=_cparams(("arbitrary",), 3 * tg * d * 4 + 2 * tg * d * 2 + tg * d * 4),
        name="moe_gather",
    )(src_tok, limit, u3)


def _tile_active(meta_ref, tile):
    return tile < meta_ref[pl.num_programs(1)]


def _expert_changed(meta_ref, tile):
    return (tile == 0) | (meta_ref[tile] != meta_ref[jnp.maximum(tile - 1, 0)])


def _moe_hidden_kernel(meta_ref, us_ref, w1_ref, w3_ref, o_ref, w1s, w3s):
    tile = pl.program_id(1)
    active = _tile_active(meta_ref, tile)

    @pl.when(active & _expert_changed(meta_ref, tile))
    def _():
        w1s[...] = w1_ref[...].astype(BF)
        w3s[...] = w3_ref[...].astype(BF)

    @pl.when(active)
    def _():
        us = us_ref[...]
        a = jnp.dot(us, w1s[...], preferred_element_type=F32)
        b = jnp.dot(us, w3s[...], preferred_element_type=F32)
        o_ref[...] = (a * _sigmoid(a) * b).astype(o_ref.dtype)

    @pl.when(jnp.logical_not(active))
    def _():
        o_ref[...] = jnp.zeros_like(o_ref)


def _moe_hidden(tile_expert, us, w1, w3, lead, tm):
    n_rows, d = us.shape
    f = w1.shape[-1]
    tn = 256
    w_spec = pl.BlockSpec((None, None, d, tn), lambda j, i, te: (lead, te[i], 0, j))
    vmem = 2 * tm * d * 2 + 4 * d * tn * 4 + 2 * d * tn * 2 + 2 * tm * tn * 2 + 4 * tm * tn * 4
    return pl.pallas_call(
        _moe_hidden_kernel,
        grid_spec=pltpu.PrefetchScalarGridSpec(
            num_scalar_prefetch=1,
            grid=(f // tn, n_rows // tm),
            in_specs=[pl.BlockSpec((tm, d), lambda j, i, te: (i, 0)), w_spec, w_spec],
            out_specs=pl.BlockSpec((tm, tn), lambda j, i, te: (i, j)),
            scratch_shapes=[pltpu.VMEM((d, tn), BF), pltpu.VMEM((d, tn), BF)],
        ),
        out_shape=jax.ShapeDtypeStruct((n_rows, f), BF),
        compiler_params=_cparams(("arbitrary", "arbitrary"), vmem),
        name="moe_hidden",
    )(tile_expert, us, w1, w3)


def _moe_out_kernel(meta_ref, h_ref, w2_ref, o_ref, w2s):
    tile = pl.program_id(1)
    active = _tile_active(meta_ref, tile)

    @pl.when(active & _expert_changed(meta_ref, tile))
    def _():
        w2s[...] = w2_ref[...].astype(BF)

    @pl.when(active)
    def _():
        o_ref[...] = jnp.dot(h_ref[...], w2s[...], preferred_element_type=F32)

    @pl.when(jnp.logical_not(active))
    def _():
        o_ref[...] = jnp.zeros_like(o_ref)


def _moe_out(tile_expert, hs, w2, lead, tm):
    n_rows, f = hs.shape
    d = w2.shape[-1]
    tn = 512
    vmem = 2 * tm * f * 2 + 2 * f * tn * 4 + f * tn * 2 + 3 * tm * tn * 4
    return pl.pallas_call(
        _moe_out_kernel,
        grid_spec=pltpu.PrefetchScalarGridSpec(
            num_scalar_prefetch=1,
            grid=(d // tn, n_rows // tm),
            in_specs=[pl.BlockSpec((tm, f), lambda j, i, te: (i, 0)),
                      pl.BlockSpec((None, None, f, tn), lambda j, i, te: (lead, te[i], 0, j))],
            out_specs=pl.BlockSpec((tm, tn), lambda j, i, te: (i, j)),
            scratch_shapes=[pltpu.VMEM((f, tn), BF)],
        ),
        out_shape=jax.ShapeDtypeStruct((n_rows, d), F32),
        compiler_params=_cparams(("arbitrary", "arbitrary"), vmem),
        name="moe_out",
    )(tile_expert, hs, w2)


def _moe_ln_kernel(p1_ref, p2_ref, x_ref, route_ref, y_hbm, gate_ref, g_ref, b_ref, *rest, emit_u):
    buf, sem = rest[-2:]
    slot = _row_gather_step((p1_ref, p2_ref), y_hbm, buf, sem, x_ref.shape[0])
    route = route_ref[...]
    f = route[:, ROUTE_W1:ROUTE_W1 + 1] * buf[slot, 0] + route[:, ROUTE_W2:ROUTE_W2 + 1] * buf[slot, 1]
    r = DN_ALPHA * x_ref[...] + gate_ref[...] * f
    _ln_tail(r, g_ref, b_ref, rest[:-2], emit_u)


def _moe_combine_ln(pos1, pos2, x2, route, ys, gate, ln_g, ln_b, seq, sc=None, sh=None):
    t, d = x2.shape
    tm = 256
    per_b = seq // tm
    emit_u = sc is not None
    row = pl.BlockSpec((tm, d), lambda i, p1, p2: (i, 0))
    vec = pl.BlockSpec((None, 1, d), lambda i, p1, p2: (i // per_b, 0, 0))
    par = pl.BlockSpec((1, d), lambda i, p1, p2: (0, 0))
    in_specs = [row, pl.BlockSpec((tm, LANES), lambda i, p1, p2: (i, 0)),
                pl.BlockSpec(memory_space=pl.ANY), vec, par, par]
    args = [x2, route, ys, gate, ln_g.reshape(1, d), ln_b.reshape(1, d)]
    out_specs = [row]
    out_shape = [jax.ShapeDtypeStruct((t, d), F32)]
    if emit_u:
        in_specs += [vec, vec]
        args += [sc, sh]
        out_specs.append(row)
        out_shape.append(jax.ShapeDtypeStruct((t, d), BF))
    outs = pl.pallas_call(
        functools.partial(_moe_ln_kernel, emit_u=emit_u),
        grid_spec=pltpu.PrefetchScalarGridSpec(
            num_scalar_prefetch=2,
            grid=(t // tm,),
            in_specs=in_specs,
            out_specs=out_specs,
            scratch_shapes=[pltpu.VMEM((2, 2, tm, d), F32), pltpu.SemaphoreType.DMA((2,))],
        ),
        out_shape=out_shape,
        compiler_params=_cparams(("arbitrary",), 4 * tm * d * 4 + 2 * tm * d * 10 + 4 * tm * d * 4),
        name="moe_combine_ln",
    )(pos1, pos2, *args)
    return outs if emit_u else (outs[0], None)


def kernel(x, c, w_ada, b_ada, w_in, attn_sinks, conv_w, conv_b, dt_bias, a_log, d_skip, ssm_norm_w,
           w_branch, w_o, ln_g, ln_b, ffn_w1, ffn_w3, ffn_w2, router_w, exp_w1, exp_w3, exp_w2):
    nb, seq, d = x.shape
    t = nb * seq
    x2 = x.reshape(t, d)
    mod = _adaln_mod(c, w_ada, b_ada).reshape(DEPTH, nb, 6, 1, d)
    u = _modulate(x2, mod[0, :, 1], mod[0, :, 0], seq)
    wt = jnp.swapaxes(w_in, 1, 2)
    for l in range(DEPTH):
        sh1, sc1, g1, sh2, sc2, g2 = (mod[l, :, i] for i in range(6))
        q2 = _matmul_wt(u, wt, l, OFF_Q, OFF_KV - OFF_Q, BF, 1024, 512, "proj_q")
        kv2 = _matmul_wt(u, wt, l, OFF_KV, OFF_Z - OFF_KV, BF, 1024, 512, "proj_kv")
        z2 = _matmul_wt(u, wt, l, OFF_Z, OFF_XBC - OFF_Z, BF, 1024, 512, "proj_z")
        xbc2 = _matmul_wt(u, wt, l, OFF_XBC, OFF_DT - OFF_XBC, BF, 1024, 512, "proj_xbc")
        dt2 = _matmul_wt(u, wt, l, OFF_DT, LANES, F32, 1024, LANES, "proj_dt")[:, :SSM_HEADS]
        ret2 = _matmul_wt(u, wt, l, OFF_TAIL, RET_WIDTH, BF, 1024, 512, "proj_ret")
        y_attn = _attention(q2, kv2, attn_sinks[l], nb, seq)
        y_ssm = _ssd(z2, xbc2, dt2, conv_w[l], conv_b[l], dt_bias[l], a_log[l], d_skip[l],
                     ssm_norm_w[l], nb, seq)
        y_ret = _retention(ret2, nb, seq)
        merged = _branch_merge(u, y_attn, y_ssm, y_ret, wt, w_branch, l)
        mix = _matmul_ws(merged, w_o, (l,), 0, d, F32, 1024, 512, "proj_o")
        sc_next, sh_next = (mod[l + 1, :, 1], mod[l + 1, :, 0]) if l + 1 < DEPTH else (None, None)
        if l % 2 == 0:
            x2, u = _residual_ln(x2, mix, g1, ln_g[l, 0], ln_b[l, 0], seq, sc2, sh2)
            tk_out = 512
            f_pad = pl.cdiv(D_FF_DENSE, tk_out) * tk_out
            hid = _swiglu_hidden(u, ffn_w1, ffn_w3, l // 2, f_pad, 1024, 256, "ffn_hidden")
            f = _matmul_k(hid, ffn_w2, (l // 2,), F32, 2048, 1024, tk_out, "ffn_out")
            x2, u = _residual_ln(x2, f, g2, ln_g[l, 1], ln_b[l, 1], seq, sc_next, sh_next)
        else:
            x2, u3, route = _residual_ln_route(x2, mix, g1, ln_g[l, 0], ln_b[l, 0], seq, sc2, sh2, router_w[l // 2])
            src_tok, meta, pos1, pos2 = _route_plan(route, MOE_TILE)
            us = _gather_rows(src_tok, meta[-1] * MOE_TILE, u3, d)
            hs = _moe_hidden(meta, us, exp_w1, exp_w3, l // 2, MOE_TILE)
            ys = _moe_out(meta, hs, exp_w2, l // 2, MOE_TILE)
            x2, u = _moe_combine_ln(pos1, pos2, x2, route, ys, g2, ln_g[l, 1], ln_b[l, 1], seq, sc_next, sh_next)
    return x2.reshape(nb, seq, d)
```
